```python
import math
import jax, jax.numpy as jnp
from jax import lax
import numpy as np

D_MODEL = 1024
BATCH = 1
SEQ = 16384
DEPTH = 2

N_MIXERS = 2
N_CONV_LAYERS = (DEPTH + N_MIXERS - 1) // N_MIXERS
N_ATTN_LAYERS = DEPTH // N_MIXERS

CONV_KERNEL = 31
CONV_EXPAND = 2

HEAD_DIM = 128
HEADS_PER_GROUP = D_MODEL // HEAD_DIM
DILATED_GROUPS = ((128, 1), (512, 4), (2048, 16))
N_GROUPS = len(DILATED_GROUPS)
N_TOTAL_HEADS = N_GROUPS * HEADS_PER_GROUP
GROUP_WIDTH = HEADS_PER_GROUP * HEAD_DIM
QKV_WIDTH = N_GROUPS * 3 * GROUP_WIDTH
BLOCK = 128
ALIBI_MAX = 8.0

D_FF = 2816
FFN_KERNEL = 3
EPS = 1e-6

kernel_name = "hybrid_conformer_dilated_attn_convffn"


def rmsnorm(x, g):
    xf = x.astype(jnp.float32)
    y = xf * lax.rsqrt(jnp.mean(xf * xf, axis=-1, keepdims=True) + EPS)
    return (y * g.astype(jnp.float32)).astype(x.dtype)


def layernorm(x, g, b):
    xf = x.astype(jnp.float32)
    mu = jnp.mean(xf, axis=-1, keepdims=True)
    var = jnp.mean(jnp.square(xf - mu), axis=-1, keepdims=True)
    y = (xf - mu) * lax.rsqrt(var + EPS)
    return (y * g.astype(jnp.float32) + b.astype(jnp.float32)).astype(x.dtype)


def causal_dwconv(x, w, b):
    k, c = w.shape
    y = lax.conv_general_dilated(
        x, w.astype(x.dtype)[:, None, :], window_strides=(1,), padding=[(k - 1, 0)],
        dimension_numbers=("NWC", "WIO", "NWC"), feature_group_count=c)
    return y + b.astype(x.dtype)


def conformer_conv_module(h, w_in, b_in, dw, dw_b, ln_g, ln_b, w_out, b_out):
    u = h @ w_in + b_in
    a, g = jnp.split(u, 2, axis=-1)
    u = a * jax.nn.sigmoid(g)
    u = causal_dwconv(u, dw, dw_b)
    u = jax.nn.silu(layernorm(u, ln_g, ln_b))
    return u @ w_out + b_out


def dilated_branch(q, k, v, slopes, window, dilation):
    b, s, h, dh = q.shape
    n_back = window // dilation
    L = s // dilation
    nb = -(-L // BLOCK)
    Lp = nb * BLOCK
    n = b * dilation

    def to_sub(t):
        t = t.reshape(b, L, dilation, h, dh).transpose(0, 2, 1, 3, 4).reshape(n, L, h, dh)
        return jnp.pad(t, ((0, 0), (0, Lp - L), (0, 0), (0, 0)))

    def windows(t):
        t = jnp.pad(t, ((0, 0), (BLOCK, 0), (0, 0), (0, 0))).reshape(n, nb + 1, BLOCK, h, dh)
        return jnp.concatenate([t[:, :-1], t[:, 1:]], axis=2)

    qb = to_sub(q).reshape(n, nb, BLOCK, h, dh).astype(jnp.float32)
    kw = windows(to_sub(k)).astype(jnp.float32)
    vw = windows(to_sub(v)).astype(jnp.float32)

    scores = jnp.einsum("nbqhd,nbkhd->nbhqk", qb, kw) * (dh ** -0.5)
    qi = jnp.arange(BLOCK)[:, None]
    ki = jnp.arange(2 * BLOCK)[None, :]
    delta = qi + BLOCK - ki
    key_idx = jnp.arange(nb)[:, None, None] * BLOCK + ki[None] - BLOCK
    valid = (delta >= 0)[None] & (delta <= n_back)[None] & (key_idx >= 0)
    bias = -slopes.astype(jnp.float32)[:, None, None] * (delta * dilation).astype(jnp.float32)[None]
    scores = jnp.where(valid[None, :, None], scores + bias[None, None], -1e30)

    m = jnp.max(scores, axis=-1, keepdims=True)
    p = jnp.exp(scores - m)
    den = jnp.sum(p, axis=-1)
    lse = (m[..., 0] + jnp.log(den)).transpose(0, 1, 3, 2)
    o = jnp.einsum("nbhqk,nbkhd->nbqhd", p, vw) / den.transpose(0, 1, 3, 2)[..., None]

    o = o.reshape(n, Lp, h, dh)[:, :L].reshape(b, dilation, L, h, dh)
    o = o.transpose(0, 2, 1, 3, 4).reshape(b, s, h, dh)
    lse = lse.reshape(n, Lp, h)[:, :L].reshape(b, dilation, L, h)
    lse = lse.transpose(0, 2, 1, 3).reshape(b, s, h)
    return o, lse


def dilated_attention_mixer(h, w_qkv, q_norm, k_norm, w_out):
    b, s, _ = h.shape
    qkv = (h @ w_qkv).reshape(b, s, N_GROUPS, 3, HEADS_PER_GROUP, HEAD_DIM)
    slopes = jnp.asarray(
        2.0 ** (-ALIBI_MAX * (np.arange(N_TOTAL_HEADS, dtype=np.float32) + 1.0) / N_TOTAL_HEADS),
        dtype=jnp.float32).reshape(N_GROUPS, HEADS_PER_GROUP)
    qn = q_norm.reshape(N_GROUPS, HEADS_PER_GROUP, HEAD_DIM)
    kn = k_norm.reshape(N_GROUPS, HEADS_PER_GROUP, HEAD_DIM)
    outs, lses = [], []
    for g, (window, dilation) in enumerate(DILATED_GROUPS):
        q = rmsnorm(qkv[:, :, g, 0], qn[g])
        k = rmsnorm(qkv[:, :, g, 1], kn[g])
        o, lse = dilated_branch(q, k, qkv[:, :, g, 2], slopes[g], window, dilation)
        outs.append(o)
        lses.append(lse)
    wts = jax.nn.softmax(jnp.stack(lses, axis=0), axis=0)
    o = jnp.sum(wts[..., None] * jnp.stack(outs, axis=0), axis=0)
    return o.astype(h.dtype).reshape(b, s, GROUP_WIDTH) @ w_out


def conv_ffn(h, w_up, dw, dw_b, w_down):
    u = causal_dwconv(h @ w_up, dw, dw_b)
    g, v = jnp.split(u, 2, axis=-1)
    return (jax.nn.silu(g) * v) @ w_down


def setup_inputs(seed: int = 0) -> dict:
    key = jax.random.key(seed)
    ks = jax.random.split(key, 24)
    f32 = jnp.float32
    nrm = lambda k, shape, scale: jax.random.normal(k, shape, f32) * scale
    D, Lc, La = D_MODEL, N_CONV_LAYERS, N_ATTN_LAYERS
    return {
        "x": jax.random.normal(ks[0], (BATCH, SEQ, D), f32),
        "norm_mix": 1.0 + nrm(ks[1], (DEPTH, D), 0.02),
        "norm_ffn": 1.0 + nrm(ks[2], (DEPTH, D), 0.02),
        "cm_w_in": nrm(ks[3], (Lc, D, CONV_EXPAND * D), D ** -0.5),
        "cm_b_in": nrm(ks[4], (Lc, CONV_EXPAND * D), 0.02),
        "cm_dw": nrm(ks[5], (Lc, CONV_KERNEL, D), CONV_KERNEL ** -0.5),
        "cm_dw_b": nrm(ks[6], (Lc, D), 0.02),
        "cm_ln_g": 1.0 + nrm(ks[7], (Lc, D), 0.02),
        "cm_ln_b": nrm(ks[8], (Lc, D), 0.02),
        "cm_w_out": nrm(ks[9], (Lc, D, D), D ** -0.5),
        "cm_b_out": nrm(ks[10], (Lc, D), 0.02),
        "at_w_qkv": nrm(ks[11], (La, D, QKV_WIDTH), D ** -0.5),
        "at_q_norm": 1.0 + nrm(ks[12], (La, N_TOTAL_HEADS, HEAD_DIM), 0.02),
        "at_k_norm": 1.0 + nrm(ks[13], (La, N_TOTAL_HEADS, HEAD_DIM), 0.02),
        "at_w_out": nrm(ks[14], (La, GROUP_WIDTH, D), GROUP_WIDTH ** -0.5),
        "ff_w_up": nrm(ks[15], (DEPTH, D, 2 * D_FF), D ** -0.5),
        "ff_dw": nrm(ks[16], (DEPTH, FFN_KERNEL, 2 * D_FF), FFN_KERNEL ** -0.5),
        "ff_dw_b": nrm(ks[17], (DEPTH, 2 * D_FF), 0.02),
        "ff_w_down": nrm(ks[18], (DEPTH, D_FF, D), D_FF ** -0.5),
    }


def reference(x, norm_mix, norm_ffn, cm_w_in, cm_b_in, cm_dw, cm_dw_b, cm_ln_g, cm_ln_b,
              cm_w_out, cm_b_out, at_w_qkv, at_q_norm, at_k_norm, at_w_out,
              ff_w_up, ff_dw, ff_dw_b, ff_w_down):
    for i in range(DEPTH):
        h = rmsnorm(x, norm_mix[i])
        j = i // N_MIXERS
        if i % N_MIXERS == 0:
            y = conformer_conv_module(h, cm_w_in[j], cm_b_in[j], cm_dw[j], cm_dw_b[j],
                                      cm_ln_g[j], cm_ln_b[j], cm_w_out[j], cm_b_out[j])
        else:
            y = dilated_attention_mixer(h, at_w_qkv[j], at_q_norm[j], at_k_norm[j], at_w_out[j])
        x = x + y
        x = x + conv_ffn(rmsnorm(x, norm_ffn[i]), ff_w_up[i], ff_dw[i], ff_dw_b[i], ff_w_down[i])
    return x
```

```python
import functools

import numpy as np
import jax
import jax.numpy as jnp
from jax import lax
from jax.experimental import pallas as pl
from jax.experimental.pallas import tpu as pltpu

F32 = jnp.float32
BF16 = jnp.bfloat16

D_MODEL = 1024
HEAD_DIM = 128
HEADS = 8
GROUP_WIDTH = HEADS * HEAD_DIM
DILATIONS = (1, 4, 16)
N_GROUPS = len(DILATIONS)
N_BACK = 128
BLOCK = 128
ATTN_TILE = BLOCK * DILATIONS[-1]
ALIBI_MAX = 8.0
CONV_KERNEL = 31
CONV_HALO = 32
FFN_KERNEL = 3
FFN_HALO = 8
D_FF = 2816
FF_CHUNK = 256
EPS = 1e-6
NEG = -1e30
SUBLANES = 8
VMEM_LIMIT = 56 * 1024 * 1024

ROW_TILE = 512
QKV_COL_TILE = 512
QKV_ROW_CHUNK = 512
NORM_ROW_CHUNK = 256


def _rms(x, gain):
    return x * lax.rsqrt(jnp.mean(x * x, axis=-1, keepdims=True) + EPS) * gain


def _const_spec(shape):
    zeros = (0,) * len(shape)
    return pl.BlockSpec(shape, lambda *_: zeros, pipeline_mode=pl.Buffered(1))


def _conformer_kernel(x_ref, gmix_ref, win_ref, bin_ref, dw_ref, dwb_ref, lng_ref, lnb_ref, wout_ref, bout_ref,
                      o_ref, buf_ref, y_ref):
    tm = x_ref.shape[0]

    @pl.when(pl.program_id(0) == 0)
    def _():
        buf_ref[0:CONV_HALO, :] = jnp.zeros((CONV_HALO, D_MODEL), F32)

    x = x_ref[...]
    h = _rms(x, gmix_ref[...]).astype(BF16)
    u = jnp.dot(h, win_ref[...], preferred_element_type=F32) + bin_ref[...]
    buf_ref[CONV_HALO:CONV_HALO + tm, :] = u[:, :D_MODEL] * jax.nn.sigmoid(u[:, D_MODEL:])

    base = CONV_HALO - SUBLANES
    rows = tm + SUBLANES

    def strip(c, carry):
        lanes = pl.ds(pl.multiple_of(c * 128, 128), 128)
        y = None
        for s in range(SUBLANES):
            p = None
            for a in range(4):
                m = SUBLANES * a + s
                if m >= CONV_KERNEL:
                    continue
                w = dw_ref[CONV_KERNEL - 1 - m:CONV_KERNEL - m, lanes]
                term = buf_ref[base - SUBLANES * a:base - SUBLANES * a + rows, lanes] * w
                p = term if p is None else p + term
            shifted = p[SUBLANES - s:SUBLANES - s + tm, :]
            y = shifted if y is None else y + shifted
        y_ref[:, lanes] = y + dwb_ref[:, lanes]
        return carry

    lax.fori_loop(0, D_MODEL // 128, strip, 0)
    buf_ref[0:CONV_HALO, :] = buf_ref[tm:tm + CONV_HALO, :]

    y = y_ref[...]
    mu = jnp.mean(y, axis=-1, keepdims=True)
    yc = y - mu
    var = jnp.mean(yc * yc, axis=-1, keepdims=True)
    z = yc * lax.rsqrt(var + EPS) * lng_ref[...] + lnb_ref[...]
    z = (z * jax.nn.sigmoid(z)).astype(BF16)
    o_ref[...] = x + jnp.dot(z, wout_ref[...], preferred_element_type=F32) + bout_ref[...]


def _conformer(x, gmix, w_in, b_in, dw, dw_b, ln_g, ln_b, w_out, b_out):
    s = x.shape[0]
    tm = ROW_TILE
    row = lambda v: v.reshape(1, -1)
    return pl.pallas_call(
        _conformer_kernel,
        grid=(s // tm,),
        in_specs=[
            pl.BlockSpec((tm, D_MODEL), lambda i: (i, 0)),
            _const_spec((1, D_MODEL)),
            _const_spec((D_MODEL, 2 * D_MODEL)),
            _const_spec((1, 2 * D_MODEL)),
            _const_spec((CONV_KERNEL, D_MODEL)),
            _const_spec((1, D_MODEL)),
            _const_spec((1, D_MODEL)),
            _const_spec((1, D_MODEL)),
            _const_spec((D_MODEL, D_MODEL)),
            _const_spec((1, D_MODEL)),
        ],
        out_specs=pl.BlockSpec((tm, D_MODEL), lambda i: (i, 0)),
        out_shape=jax.ShapeDtypeStruct((s, D_MODEL), F32),
        scratch_shapes=[
            pltpu.VMEM((CONV_HALO + tm, D_MODEL), F32),
            pltpu.VMEM((tm, D_MODEL), F32),
        ],
        compiler_params=pltpu.CompilerParams(
            dimension_semantics=("arbitrary",), vmem_limit_bytes=VMEM_LIMIT),
        name="conformer",
    )(x, row(gmix), w_in.astype(BF16), row(b_in), dw, row(dw_b), row(ln_g), row(ln_b),
      w_out.astype(BF16), row(b_out))


def _ffn_kernel(*refs, with_attn):
    if with_attn:
        x_ref, a_ref, wattn_ref, g_ref, wup_ref, dw_ref, dwb_ref, wdown_ref, o_ref, ubuf_ref, carry_ref, h_ref = refs
    else:
        x_ref, g_ref, wup_ref, dw_ref, dwb_ref, wdown_ref, o_ref, ubuf_ref, carry_ref, h_ref = refs
    tm = x_ref.shape[0]
    n_chunks = D_FF // FF_CHUNK
    width = 2 * FF_CHUNK

    @pl.when(pl.program_id(0) == 0)
    def _():
        carry_ref[...] = jnp.zeros(carry_ref.shape, F32)

    x = x_ref[...]
    if with_attn:
        x = x + jnp.dot(a_ref[...], wattn_ref[...], preferred_element_type=F32)
    o_ref[...] = x
    h_ref[...] = _rms(x, g_ref[...]).astype(BF16)

    def chunk(c, carry):
        cols = pl.ds(pl.multiple_of(c * width, width), width)
        u = jnp.dot(h_ref[...], wup_ref[:, cols], preferred_element_type=F32)
        ubuf_ref[0:FFN_HALO, :] = carry_ref[c]
        ubuf_ref[FFN_HALO:FFN_HALO + tm, :] = u
        carry_ref[c] = u[tm - FFN_HALO:tm, :]
        y = dwb_ref[:, cols]
        for j in range(FFN_KERNEL):
            back = FFN_KERNEL - 1 - j
            y = y + ubuf_ref[FFN_HALO - back:FFN_HALO - back + tm, :] * dw_ref[j:j + 1, cols]
        gate = y[:, :FF_CHUNK]
        act = (gate * jax.nn.sigmoid(gate) * y[:, FF_CHUNK:]).astype(BF16)
        rows = pl.ds(pl.multiple_of(c * FF_CHUNK, FF_CHUNK), FF_CHUNK)
        o_ref[...] += jnp.dot(act, wdown_ref[rows, :], preferred_element_type=F32)
        return carry

    lax.fori_loop(0, n_chunks, chunk, 0)


def _interleave_gate_value(w):
    lead = w.shape[:-1]
    w = w.reshape(*lead, 2, D_FF // FF_CHUNK, FF_CHUNK)
    return jnp.swapaxes(w, -3, -2).reshape(*lead, 2 * D_FF)


def _ffn(x, gain, w_up, dw, dw_b, w_down, attn=None, w_attn=None):
    s = x.shape[0]
    tm = ROW_TILE
    with_attn = attn is not None
    row_spec = pl.BlockSpec((tm, D_MODEL), lambda i: (i, 0))
    args, specs = [x], [row_spec]
    if with_attn:
        args += [attn, w_attn.astype(BF16)]
        specs += [pl.BlockSpec((tm, GROUP_WIDTH), lambda i: (i, 0)), _const_spec((GROUP_WIDTH, D_MODEL))]
    args += [gain.reshape(1, -1), _interleave_gate_value(w_up).astype(BF16), _interleave_gate_value(dw),
             _interleave_gate_value(dw_b).reshape(1, -1), w_down.astype(BF16)]
    specs += [_const_spec((1, D_MODEL)), _const_spec((D_MODEL, 2 * D_FF)), _const_spec((FFN_KERNEL, 2 * D_FF)),
              _const_spec((1, 2 * D_FF)), _const_spec((D_FF, D_MODEL))]
    return pl.pallas_call(
        functools.partial(_ffn_kernel, with_attn=with_attn),
        grid=(s // tm,),
        in_specs=specs,
        out_specs=row_spec,
        out_shape=jax.ShapeDtypeStruct((s, D_MODEL), F32),
        scratch_shapes=[
            pltpu.VMEM((FFN_HALO + tm, 2 * FF_CHUNK), F32),
            pltpu.VMEM((D_FF // FF_CHUNK, FFN_HALO, 2 * FF_CHUNK), F32),
            pltpu.VMEM((tm, D_MODEL), BF16),
        ],
        compiler_params=pltpu.CompilerParams(
            dimension_semantics=("arbitrary",), vmem_limit_bytes=VMEM_LIMIT),
        name="ffn_attn" if with_attn else "ffn",
    )(*args)


def _qkv_kernel(x_ref, g_ref, w_ref, gain_ref, o_ref, hp_ref, hs_ref):
    j = pl.program_id(1)
    tiles_per_group = 3 * GROUP_WIDTH // QKV_COL_TILE
    tiles_per_kind = GROUP_WIDTH // QKV_COL_TILE

    @pl.when(j == 0)
    def _():
        n_strips = D_MODEL // 128
        for c in range(ATTN_TILE // NORM_ROW_CHUNK):
            rows = slice(c * NORM_ROW_CHUNK, (c + 1) * NORM_ROW_CHUNK)
            h = _rms(x_ref[rows, :], g_ref[...])
            hp_ref[0, rows, :] = h.astype(BF16)
            for k in range(n_strips):
                hs_ref[k, rows, :] = h[:, k * 128:(k + 1) * 128]
        for g, d in enumerate(DILATIONS):
            if d == 1:
                continue
            span = BLOCK * d
            for b in range(ATTN_TILE // span):
                for r in range(d):
                    dst = slice(b * span + r * BLOCK, b * span + (r + 1) * BLOCK)
                    for k in range(n_strips):
                        src = hs_ref[k, pl.ds(b * span + r, BLOCK, stride=d), :]
                        hp_ref[g, dst, k * 128:(k + 1) * 128] = src.astype(BF16)

    grp = j // tiles_per_group
    kind = (j % tiles_per_group) // tiles_per_kind

    def chunk(c, carry):
        rows = pl.ds(pl.multiple_of(c * QKV_ROW_CHUNK, QKV_ROW_CHUNK), QKV_ROW_CHUNK)
        acc = jnp.dot(hp_ref[grp, rows, :], w_ref[...], preferred_element_type=F32)

        @pl.when(kind < 2)
        def _():
            for hh in range(QKV_COL_TILE // HEAD_DIM):
                cols = slice(hh * HEAD_DIM, (hh + 1) * HEAD_DIM)
                o_ref[rows, cols] = _rms(acc[:, cols], gain_ref[:, cols]).astype(BF16)

        @pl.when(kind == 2)
        def _():
            o_ref[rows, :] = acc.astype(BF16)

        return carry

    lax.fori_loop(0, ATTN_TILE // QKV_ROW_CHUNK, chunk, 0)


def _qkv(x, gain, w_qkv, q_norm, k_norm):
    s = x.shape[0]
    width = N_GROUPS * 3 * GROUP_WIDTH
    qn = q_norm.reshape(N_GROUPS, 1, GROUP_WIDTH) * (HEAD_DIM ** -0.5)
    kn = k_norm.reshape(N_GROUPS, 1, GROUP_WIDTH)
    head_gain = jnp.concatenate([qn, kn, jnp.ones_like(kn)], axis=1).reshape(1, width)
    return pl.pallas_call(
        _qkv_kernel,
        grid=(s // ATTN_TILE, width // QKV_COL_TILE),
        in_specs=[
            pl.BlockSpec((ATTN_TILE, D_MODEL), lambda i, j: (i, 0)),
            pl.BlockSpec((1, D_MODEL), lambda i, j: (0, 0)),
            pl.BlockSpec((D_MODEL, QKV_COL_TILE), lambda i, j: (0, j)),
            pl.BlockSpec((1, QKV_COL_TILE), lambda i, j: (0, j)),
        ],
        out_specs=pl.BlockSpec((ATTN_TILE, QKV_COL_TILE), lambda i, j: (i, j)),
        out_shape=jax.ShapeDtypeStruct((s, width), BF16),
        scratch_shapes=[
            pltpu.VMEM((N_GROUPS, ATTN_TILE, D_MODEL), BF16),
            pltpu.VMEM((D_MODEL // 128, ATTN_TILE, 128), F32),
        ],
        compiler_params=pltpu.CompilerParams(
            dimension_semantics=("arbitrary", "arbitrary"), vmem_limit_bytes=VMEM_LIMIT),
        name="qkv",
    )(x, gain.reshape(1, -1), w_qkv.astype(BF16), head_gain)


def _attn_kernel(slopes_ref, *refs):
    q_refs, kc_refs, kp_refs, vc_refs, vp_refs = (refs[3 * n:3 * n + 3] for n in range(5))
    o_ref, og_ref, lg_ref = refs[15:]
    tile = pl.program_id(0)
    head = pl.program_id(1)

    qi = lax.broadcasted_iota(jnp.int32, (BLOCK, 2 * BLOCK), 0)
    ki = lax.broadcasted_iota(jnp.int32, (BLOCK, 2 * BLOCK), 1)
    delta = qi + BLOCK - ki
    in_window = (delta >= 0) & (delta <= N_BACK)
    has_prev = in_window & ((ki >= BLOCK) | (tile > 0))

    for g, d in enumerate(DILATIONS):
        bias = -slopes_ref[g, head] * (delta * d).astype(F32)
        n_blocks = ATTN_TILE // BLOCK
        n_from_prev = min(d, n_blocks)
        q_ref, kc_ref, kp_ref, vc_ref, vp_ref = q_refs[g], kc_refs[g], kp_refs[g], vc_refs[g], vp_refs[g]

        def block(n, carry, *, from_prev):
            row = pl.multiple_of(n * BLOCK, BLOCK)
            if from_prev:
                prev_k, prev_v = kp_ref[pl.ds(row, BLOCK), :], vp_ref[pl.ds(row, BLOCK), :]
                valid = has_prev
            else:
                back = pl.multiple_of(row - BLOCK * d, BLOCK)
                prev_k, prev_v = kc_ref[pl.ds(back, BLOCK), :], vc_ref[pl.ds(back, BLOCK), :]
                valid = in_window
            k2 = jnp.concatenate([prev_k, kc_ref[pl.ds(row, BLOCK), :]], axis=0)
            v2 = jnp.concatenate([prev_v, vc_ref[pl.ds(row, BLOCK), :]], axis=0)
            sc = lax.dot_general(q_ref[pl.ds(row, BLOCK), :], k2, (((1,), (1,)), ((), ())),
                                 preferred_element_type=F32)
            sc = jnp.where(valid, sc + bias, NEG)
            m = jnp.max(sc, axis=-1, keepdims=True)
            p = jnp.exp(sc - m)
            den = jnp.sum(p, axis=-1, keepdims=True)
            o = jnp.dot(p.astype(BF16), v2, preferred_element_type=F32) / den
            lse = jnp.broadcast_to(m + jnp.log(den), (BLOCK, HEAD_DIM))
            if d == 1:
                dst = pl.ds(row, BLOCK)
            else:
                dst = pl.ds((n // d) * (BLOCK * d) + n % d, BLOCK, stride=d)
            og_ref[g, dst, :] = o
            lg_ref[g, dst, :] = lse
            return carry

        lax.fori_loop(0, n_from_prev, functools.partial(block, from_prev=True), 0)
        if n_from_prev < n_blocks:
            lax.fori_loop(n_from_prev, n_blocks, functools.partial(block, from_prev=False), 0)

    lse = lg_ref[...]
    w = jnp.exp(lse - jnp.max(lse, axis=0, keepdims=True))
    o_ref[...] = (jnp.sum(w * og_ref[...], axis=0) / jnp.sum(w, axis=0)).astype(o_ref.dtype)


def _attention(qkv):
    s = qkv.shape[0]
    slopes = jnp.asarray(
        2.0 ** (-ALIBI_MAX * (np.arange(N_GROUPS * HEADS, dtype=np.float32) + 1.0) / (N_GROUPS * HEADS)),
        dtype=F32).reshape(N_GROUPS, HEADS)

    def col(g, kind):
        return lambda t, h: (t, (3 * g + kind) * HEADS + h)

    def prev_col(g, kind):
        ratio = ATTN_TILE // (BLOCK * DILATIONS[g])
        return lambda t, h: (jnp.maximum(t * ratio - 1, 0), (3 * g + kind) * HEADS + h)

    cur = lambda kind: [pl.BlockSpec((ATTN_TILE, HEAD_DIM), col(g, kind)) for g in range(N_GROUPS)]
    prev = lambda kind: [pl.BlockSpec((BLOCK * DILATIONS[g], HEAD_DIM), prev_col(g, kind)) for g in range(N_GROUPS)]
    in_specs = ([pl.BlockSpec(memory_space=pltpu.SMEM)] + cur(0) + cur(1) + prev(1) + cur(2) + prev(2))
    return pl.pallas_call(
        _attn_kernel,
        grid=(s // ATTN_TILE, HEADS),
        in_specs=in_specs,
        out_specs=pl.BlockSpec((ATTN_TILE, HEAD_DIM), lambda t, h: (t, h)),
        out_shape=jax.ShapeDtypeStruct((s, GROUP_WIDTH), BF16),
        scratch_shapes=[
            pltpu.VMEM((N_GROUPS, ATTN_TILE, HEAD_DIM), F32),
            pltpu.VMEM((N_GROUPS, ATTN_TILE, HEAD_DIM), F32),
        ],
        compiler_params=pltpu.CompilerParams(
            dimension_semantics=("arbitrary", "arbitrary"), vmem_limit_bytes=VMEM_LIMIT),
        name="dilated_attention",
    )(slopes, *([qkv] * 15))


def kernel(x, norm_mix, norm_ffn, cm_w_in, cm_b_in, cm_dw, cm_dw_b, cm_ln_g, cm_ln_b, cm_w_out, cm_b_out,
           at_w_qkv, at_q_norm, at_k_norm, at_w_out, ff_w_up, ff_dw, ff_dw_b, ff_w_down):
    batch, seq, _ = x.shape
    assert seq % ATTN_TILE == 0 and x.shape[-1] == D_MODEL
    outs = []
    for b in range(batch):
        h = x[b]
        h = _conformer(h, norm_mix[0], cm_w_in[0], cm_b_in[0], cm_dw[0], cm_dw_b[0], cm_ln_g[0], cm_ln_b[0],
                       cm_w_out[0], cm_b_out[0])
        h = _ffn(h, norm_ffn[0], ff_w_up[0], ff_dw[0], ff_dw_b[0], ff_w_down[0])
        qkv = _qkv(h, norm_mix[1], at_w_qkv[0], at_q_norm[0], at_k_norm[0])
        attn = _attention(qkv)
        h = _ffn(h, norm_ffn[1], ff_w_up[1], ff_dw[1], ff_dw_b[1], ff_w_down[1], attn=attn, w_attn=at_w_out[0])
        outs.append(h)
    return jnp.stack(outs, axis=0)
```

```python
import functools

import numpy as np
import jax
import jax.numpy as jnp
from jax import lax
from jax.experimental import pallas as pl
from jax.experimental.pallas import tpu as pltpu

F32 = jnp.float32
BF16 = jnp.bfloat16

D_MODEL = 1024
HEAD_DIM = 128
HEADS = 8
GROUP_WIDTH = HEADS * HEAD_DIM
DILATIONS = (1, 4, 16)
N_GROUPS = len(DILATIONS)
N_BACK = 128
BLOCK = 128
ATTN_TILE = BLOCK * DILATIONS[-1]
ALIBI_MAX = 8.0
CONV_KERNEL = 31
CONV_HALO = 32
FFN_KERNEL = 3
FFN_HALO = 8
D_FF = 2816
FF_CHUNK = 256
EPS = 1e-6
NEG = -1e30
SUBLANES = 8
VMEM_LIMIT = 56 * 1024 * 1024

ROW_TILE = 512
QKV_COL_TILE = 512
QKV_ROW_CHUNK = 512
NORM_ROW_CHUNK = 256
ATTN_UNROLL = 4


def _rms(x, gain):
    return x * lax.rsqrt(jnp.mean(x * x, axis=-1, keepdims=True) + EPS) * gain


def _const_spec(shape):
    zeros = (0,) * len(shape)
    return pl.BlockSpec(shape, lambda *_: zeros, pipeline_mode=pl.Buffered(1))


def _conformer_kernel(x_ref, gmix_ref, win_ref, bin_ref, dw_ref, dwb_ref, lng_ref, lnb_ref, wout_ref, bout_ref,
                      o_ref, buf_ref, y_ref):
    tm = x_ref.shape[0]

    @pl.when(pl.program_id(0) == 0)
    def _():
        buf_ref[0:CONV_HALO, :] = jnp.zeros((CONV_HALO, D_MODEL), F32)

    x = x_ref[...]
    h = _rms(x, gmix_ref[...]).astype(BF16)
    u = jnp.dot(h, win_ref[...], preferred_element_type=F32) + bin_ref[...]
    buf_ref[CONV_HALO:CONV_HALO + tm, :] = u[:, :D_MODEL] * jax.nn.sigmoid(u[:, D_MODEL:])

    base = CONV_HALO - SUBLANES
    rows = tm + SUBLANES

    def strip(c, carry):
        lanes = pl.ds(pl.multiple_of(c * 128, 128), 128)
        y = None
        for s in range(SUBLANES):
            p = None
            for a in range(4):
                m = SUBLANES * a + s
                if m >= CONV_KERNEL:
                    continue
                w = dw_ref[CONV_KERNEL - 1 - m:CONV_KERNEL - m, lanes]
                term = buf_ref[base - SUBLANES * a:base - SUBLANES * a + rows, lanes] * w
                p = term if p is None else p + term
            shifted = p[SUBLANES - s:SUBLANES - s + tm, :]
            y = shifted if y is None else y + shifted
        y_ref[:, lanes] = y + dwb_ref[:, lanes]
        return carry

    lax.fori_loop(0, D_MODEL // 128, strip, 0)
    buf_ref[0:CONV_HALO, :] = buf_ref[tm:tm + CONV_HALO, :]

    y = y_ref[...]
    mu = jnp.mean(y, axis=-1, keepdims=True)
    yc = y - mu
    var = jnp.mean(yc * yc, axis=-1, keepdims=True)
    z = yc * lax.rsqrt(var + EPS) * lng_ref[...] + lnb_ref[...]
    z = (z * jax.nn.sigmoid(z)).astype(BF16)
    o_ref[...] = x + jnp.dot(z, wout_ref[...], preferred_element_type=F32) + bout_ref[...]


def _conformer(x, gmix, w_in, b_in, dw, dw_b, ln_g, ln_b, w_out, b_out):
    s = x.shape[0]
    tm = ROW_TILE
    row = lambda v: v.reshape(1, -1)
    return pl.pallas_call(
        _conformer_kernel,
        grid=(s // tm,),
        in_specs=[
            pl.BlockSpec((tm, D_MODEL), lambda i: (i, 0)),
            _const_spec((1, D_MODEL)),
            _const_spec((D_MODEL, 2 * D_MODEL)),
            _const_spec((1, 2 * D_MODEL)),
            _const_spec((CONV_KERNEL, D_MODEL)),
            _const_spec((1, D_MODEL)),
            _const_spec((1, D_MODEL)),
            _const_spec((1, D_MODEL)),
            _const_spec((D_MODEL, D_MODEL)),
            _const_spec((1, D_MODEL)),
        ],
        out_specs=pl.BlockSpec((tm, D_MODEL), lambda i: (i, 0)),
        out_shape=jax.ShapeDtypeStruct((s, D_MODEL), F32),
        scratch_shapes=[
            pltpu.VMEM((CONV_HALO + tm, D_MODEL), F32),
            pltpu.VMEM((tm, D_MODEL), F32),
        ],
        compiler_params=pltpu.CompilerParams(
            dimension_semantics=("arbitrary",), vmem_limit_bytes=VMEM_LIMIT),
        name="conformer",
    )(x, row(gmix), w_in.astype(BF16), row(b_in), dw, row(dw_b), row(ln_g), row(ln_b),
      w_out.astype(BF16), row(b_out))


def _ffn_kernel(*refs, with_attn):
    if with_attn:
        x_ref, a_ref, wattn_ref, *refs = refs
    else:
        x_ref, *refs = refs
    g_ref, wup_ref, dw_ref, dwb_ref, wdown_ref, o_ref, ubuf_a, ubuf_b, carry_ref, h_ref = refs
    tm = x_ref.shape[0]
    n_chunks = D_FF // FF_CHUNK

    @pl.when(pl.program_id(0) == 0)
    def _():
        carry_ref[...] = jnp.zeros(carry_ref.shape, F32)

    x = x_ref[...]
    if with_attn:
        x = x + jnp.dot(a_ref[...], wattn_ref[...], preferred_element_type=F32)
    o_ref[...] = x
    h_ref[...] = _rms(x, g_ref[...]).astype(BF16)

    def gate_value(ref, rows, c):
        gate = pl.ds(pl.multiple_of(c * FF_CHUNK, FF_CHUNK), FF_CHUNK)
        value = pl.ds(pl.multiple_of(D_FF + c * FF_CHUNK, FF_CHUNK), FF_CHUNK)
        return jnp.concatenate([ref[rows, gate], ref[rows, value]], axis=1)

    def up(c, ubuf_ref):
        u = jnp.dot(h_ref[...], gate_value(wup_ref, slice(None), c), preferred_element_type=F32)
        ubuf_ref[0:FFN_HALO, :] = carry_ref[c]
        ubuf_ref[FFN_HALO:FFN_HALO + tm, :] = u
        carry_ref[c] = u[tm - FFN_HALO:tm, :]

    def down(c, ubuf_ref):
        y = gate_value(dwb_ref, slice(None), c)
        for j in range(FFN_KERNEL):
            back = FFN_KERNEL - 1 - j
            y = y + ubuf_ref[FFN_HALO - back:FFN_HALO - back + tm, :] * gate_value(dw_ref, slice(j, j + 1), c)
        gate = y[:, :FF_CHUNK]
        act = (gate * jax.nn.sigmoid(gate) * y[:, FF_CHUNK:]).astype(BF16)
        rows = pl.ds(pl.multiple_of(c * FF_CHUNK, FF_CHUNK), FF_CHUNK)
        o_ref[...] += jnp.dot(act, wdown_ref[rows, :], preferred_element_type=F32)

    up(0, ubuf_a)

    def pair(it, carry):
        c = 2 * it
        up(c + 1, ubuf_b)
        down(c, ubuf_a)
        up(c + 2, ubuf_a)
        down(c + 1, ubuf_b)
        return carry

    assert n_chunks % 2 == 1
    lax.fori_loop(0, n_chunks // 2, pair, 0)
    down(n_chunks - 1, ubuf_a)


def _ffn(x, gain, w_up, dw, dw_b, w_down, attn=None, w_attn=None):
    s = x.shape[0]
    tm = ROW_TILE
    with_attn = attn is not None
    row_spec = pl.BlockSpec((tm, D_MODEL), lambda i: (i, 0))
    args, specs = [x], [row_spec]
    if with_attn:
        args += [attn, w_attn.astype(BF16)]
        specs += [pl.BlockSpec((tm, GROUP_WIDTH), lambda i: (i, 0)), _const_spec((GROUP_WIDTH, D_MODEL))]
    args += [gain.reshape(1, -1), w_up.astype(BF16), dw, dw_b.reshape(1, -1), w_down.astype(BF16)]
    specs += [_const_spec((1, D_MODEL)), _const_spec((D_MODEL, 2 * D_FF)), _const_spec((FFN_KERNEL, 2 * D_FF)),
              _const_spec((1, 2 * D_FF)), _const_spec((D_FF, D_MODEL))]
    return pl.pallas_call(
        functools.partial(_ffn_kernel, with_attn=with_attn),
        grid=(s // tm,),
        in_specs=specs,
        out_specs=row_spec,
        out_shape=jax.ShapeDtypeStruct((s, D_MODEL), F32),
        scratch_shapes=[
            pltpu.VMEM((FFN_HALO + tm, 2 * FF_CHUNK), F32),
            pltpu.VMEM((FFN_HALO + tm, 2 * FF_CHUNK), F32),
            pltpu.VMEM((D_FF // FF_CHUNK, FFN_HALO, 2 * FF_CHUNK), F32),
            pltpu.VMEM((tm, D_MODEL), BF16),
        ],
        compiler_params=pltpu.CompilerParams(
            dimension_semantics=("arbitrary",), vmem_limit_bytes=VMEM_LIMIT),
        name="ffn_attn" if with_attn else "ffn",
    )(*args)


def _qkv_kernel(x_ref, g_ref, w_ref, gain_ref, o_ref, hp_ref, hs_ref):
    j = pl.program_id(1)
    tiles_per_group = 3 * GROUP_WIDTH // QKV_COL_TILE
    tiles_per_kind = GROUP_WIDTH // QKV_COL_TILE

    @pl.when(j == 0)
    def _():
        n_strips = D_MODEL // 128
        for c in range(ATTN_TILE // NORM_ROW_CHUNK):
            rows = slice(c * NORM_ROW_CHUNK, (c + 1) * NORM_ROW_CHUNK)
            h = _rms(x_ref[rows, :], g_ref[...])
            hp_ref[0, rows, :] = h.astype(BF16)
            for k in range(n_strips):
                hs_ref[k, rows, :] = h[:, k * 128:(k + 1) * 128]
        for g, d in enumerate(DILATIONS):
            if d == 1:
                continue
            span = BLOCK * d
            for b in range(ATTN_TILE // span):
                for r in range(d):
                    dst = slice(b * span + r * BLOCK, b * span + (r + 1) * BLOCK)
                    for k in range(n_strips):
                        src = hs_ref[k, pl.ds(b * span + r, BLOCK, stride=d), :]
                        hp_ref[g, dst, k * 128:(k + 1) * 128] = src.astype(BF16)

    grp = j // tiles_per_group
    kind = (j % tiles_per_group) // tiles_per_kind

    is_normed = kind < 2
    for c in range(ATTN_TILE // QKV_ROW_CHUNK):
        rows = slice(c * QKV_ROW_CHUNK, (c + 1) * QKV_ROW_CHUNK)
        acc = jnp.dot(hp_ref[grp, rows, :], w_ref[...], preferred_element_type=F32)
        for hh in range(QKV_COL_TILE // HEAD_DIM):
            cols = slice(hh * HEAD_DIM, (hh + 1) * HEAD_DIM)
            a = acc[:, cols]
            inv = lax.rsqrt(jnp.mean(a * a, axis=-1, keepdims=True) + EPS)
            o_ref[rows, cols] = (a * jnp.where(is_normed, inv, 1.0) * gain_ref[:, cols]).astype(BF16)


def _qkv(x, gain, w_qkv, q_norm, k_norm):
    s = x.shape[0]
    width = N_GROUPS * 3 * GROUP_WIDTH
    qn = q_norm.reshape(N_GROUPS, 1, GROUP_WIDTH) * (HEAD_DIM ** -0.5)
    kn = k_norm.reshape(N_GROUPS, 1, GROUP_WIDTH)
    head_gain = jnp.concatenate([qn, kn, jnp.ones_like(kn)], axis=1).reshape(1, width)
    return pl.pallas_call(
        _qkv_kernel,
        grid=(s // ATTN_TILE, width // QKV_COL_TILE),
        in_specs=[
            pl.BlockSpec((ATTN_TILE, D_MODEL), lambda i, j: (i, 0)),
            pl.BlockSpec((1, D_MODEL), lambda i, j: (0, 0)),
            pl.BlockSpec((D_MODEL, QKV_COL_TILE), lambda i, j: (0, j)),
            pl.BlockSpec((1, QKV_COL_TILE), lambda i, j: (0, j)),
        ],
        out_specs=pl.BlockSpec((ATTN_TILE, QKV_COL_TILE), lambda i, j: (i, j)),
        out_shape=jax.ShapeDtypeStruct((s, width), BF16),
        scratch_shapes=[
            pltpu.VMEM((N_GROUPS, ATTN_TILE, D_MODEL), BF16),
            pltpu.VMEM((D_MODEL // 128, ATTN_TILE, 128), F32),
        ],
        compiler_params=pltpu.CompilerParams(
            dimension_semantics=("arbitrary", "arbitrary"), vmem_limit_bytes=VMEM_LIMIT),
        name="qkv",
    )(x, gain.reshape(1, -1), w_qkv.astype(BF16), head_gain)


def _attn_kernel(slopes_ref, *refs):
    q_refs, kc_refs, kp_refs, vc_refs, vp_refs = (refs[3 * n:3 * n + 3] for n in range(5))
    o_ref, og_ref, lg_ref = refs[15:]
    tile = pl.program_id(0)
    head = pl.program_id(1)

    qi = lax.broadcasted_iota(jnp.int32, (BLOCK, 2 * BLOCK), 0)
    ki = lax.broadcasted_iota(jnp.int32, (BLOCK, 2 * BLOCK), 1)
    delta = qi + BLOCK - ki
    in_window = (delta >= 0) & (delta <= N_BACK)
    has_prev = in_window & ((ki >= BLOCK) | (tile > 0))

    ones = jnp.ones((2 * BLOCK, HEAD_DIM), BF16)

    for g, d in enumerate(DILATIONS):
        bias = -slopes_ref[g, head] * (delta * d).astype(F32)
        n_blocks = ATTN_TILE // BLOCK
        n_from_prev = min(d, n_blocks)
        q_ref, kc_ref, kp_ref, vc_ref, vp_ref = q_refs[g], kc_refs[g], kp_refs[g], vc_refs[g], vp_refs[g]

        def block(n, *, from_prev):
            row = pl.multiple_of(n * BLOCK, BLOCK)
            if from_prev:
                prev_k, prev_v = kp_ref[pl.ds(row, BLOCK), :], vp_ref[pl.ds(row, BLOCK), :]
                valid = has_prev
            else:
                back = pl.multiple_of(row - BLOCK * d, BLOCK)
                prev_k, prev_v = kc_ref[pl.ds(back, BLOCK), :], vc_ref[pl.ds(back, BLOCK), :]
                valid = in_window
            k2 = jnp.concatenate([prev_k, kc_ref[pl.ds(row, BLOCK), :]], axis=0)
            v2 = jnp.concatenate([prev_v, vc_ref[pl.ds(row, BLOCK), :]], axis=0)
            sc = lax.dot_general(q_ref[pl.ds(row, BLOCK), :], k2, (((1,), (1,)), ((), ())),
                                 preferred_element_type=F32)
            sc = jnp.where(valid, sc + bias, NEG)
            m = jnp.max(sc, axis=-1, keepdims=True)
            p = jnp.exp(sc - m).astype(BF16)
            od = jnp.dot(p, jnp.concatenate([v2, ones], axis=1), preferred_element_type=F32)
            den = od[:, HEAD_DIM:]
            if d == 1:
                dst = pl.ds(row, BLOCK)
            else:
                shift = d.bit_length() - 1
                dst = pl.ds(lax.shift_right_logical(n, shift) * (BLOCK * d) + (n & (d - 1)), BLOCK, stride=d)
            og_ref[g, dst, :] = od[:, :HEAD_DIM] / den
            lg_ref[g, dst, :] = m + jnp.log(den)

        def run(lo, hi, from_prev):
            count = hi - lo
            unroll = next(u for u in (ATTN_UNROLL, 5, 3, 2, 1) if count % u == 0)

            def body(it, carry):
                for u in range(unroll):
                    block(lo + it * unroll + u, from_prev=from_prev)
                return carry

            lax.fori_loop(0, count // unroll, body, 0)

        run(0, n_from_prev, True)
        if n_from_prev < n_blocks:
            run(n_from_prev, n_blocks, False)

    def merge(c, carry):
        rows = pl.ds(pl.multiple_of(c * BLOCK, BLOCK), BLOCK)
        lse = [lg_ref[g, rows, :] for g in range(N_GROUPS)]
        top = functools.reduce(jnp.maximum, lse)
        w = [jnp.exp(l - top) for l in lse]
        num = sum(w[g] * og_ref[g, rows, :] for g in range(N_GROUPS))
        o_ref[rows, :] = (num / sum(w)).astype(o_ref.dtype)
        return carry

    lax.fori_loop(0, ATTN_TILE // BLOCK, merge, 0)


def _attention(qkv):
    s = qkv.shape[0]
    slopes = jnp.asarray(
        2.0 ** (-ALIBI_MAX * (np.arange(N_GROUPS * HEADS, dtype=np.float32) + 1.0) / (N_GROUPS * HEADS)),
        dtype=F32).reshape(N_GROUPS, HEADS)

    def col(g, kind):
        return lambda t, h: (t, (3 * g + kind) * HEADS + h)

    def prev_col(g, kind):
        ratio = ATTN_TILE // (BLOCK * DILATIONS[g])
        return lambda t, h: (jnp.maximum(t * ratio - 1, 0), (3 * g + kind) * HEADS + h)

    cur = lambda kind: [pl.BlockSpec((ATTN_TILE, HEAD_DIM), col(g, kind)) for g in range(N_GROUPS)]
    prev = lambda kind: [pl.BlockSpec((BLOCK * DILATIONS[g], HEAD_DIM), prev_col(g, kind)) for g in range(N_GROUPS)]
    in_specs = ([pl.BlockSpec(memory_space=pltpu.SMEM)] + cur(0) + cur(1) + prev(1) + cur(2) + prev(2))
    return pl.pallas_call(
        _attn_kernel,
        grid=(s // ATTN_TILE, HEADS),
        in_specs=in_specs,
        out_specs=pl.BlockSpec((ATTN_TILE, HEAD_DIM), lambda t, h: (t, h)),
        out_shape=jax.ShapeDtypeStruct((s, GROUP_WIDTH), BF16),
        scratch_shapes=[
            pltpu.VMEM((N_GROUPS, ATTN_TILE, HEAD_DIM), F32),
            pltpu.VMEM((N_GROUPS, ATTN_TILE, HEAD_DIM), F32),
        ],
        compiler_params=pltpu.CompilerParams(
            dimension_semantics=("arbitrary", "arbitrary"), vmem_limit_bytes=VMEM_LIMIT),
        name="dilated_attention",
    )(slopes, *([qkv] * 15))


def kernel(x, norm_mix, norm_ffn, cm_w_in, cm_b_in, cm_dw, cm_dw_b, cm_ln_g, cm_ln_b, cm_w_out, cm_b_out,
           at_w_qkv, at_q_norm, at_k_norm, at_w_out, ff_w_up, ff_dw, ff_dw_b, ff_w_down):
    batch, seq, _ = x.shape
    assert seq % ATTN_TILE == 0 and x.shape[-1] == D_MODEL
    outs = []
    for b in range(batch):
        h = x[b]
        h = _conformer(h, norm_mix[0], cm_w_in[0], cm_b_in[0], cm_dw[0], cm_dw_b[0], cm_ln_g[0], cm_ln_b[0],
                       cm_w_out[0], cm_b_out[0])
        h = _ffn(h, norm_ffn[0], ff_w_up[0], ff_dw[0], ff_dw_b[0], ff_w_down[0])
        qkv = _qkv(h, norm_mix[1], at_w_qkv[0], at_q_norm[0], at_k_norm[0])
        attn = _attention(qkv)
        h = _ffn(h, norm_ffn[1], ff_w_up[1], ff_dw[1], ff_dw_b[1], ff_w_down[1], attn=attn, w_attn=at_w_out[0])
        outs.append(h)
    return jnp.stack(outs, axis=0)
```

```python
import functools

import numpy as np
import jax
import jax.numpy as jnp
from jax import lax
from jax.experimental import pallas as pl
from jax.experimental.pallas import tpu as pltpu

F32 = jnp.float32
BF16 = jnp.bfloat16

D_MODEL = 1024
HEAD_DIM = 128
HEADS = 8
GROUP_WIDTH = HEADS * HEAD_DIM
DILATIONS = (1, 4, 16)
N_GROUPS = len(DILATIONS)
N_BACK = 128
BLOCK = 128
ATTN_TILE = BLOCK * DILATIONS[-1]
ALIBI_MAX = 8.0
CONV_KERNEL = 31
CONV_HALO = 32
FFN_KERNEL = 3
FFN_HALO = 8
D_FF = 2816
FF_CHUNK = 256
EPS = 1e-6
NEG = -1e30
SUBLANES = 8
VMEM_LIMIT = 56 * 1024 * 1024

ROW_TILE = 512
QKV_COL_TILE = 512
QKV_ROW_CHUNK = 512
NORM_ROW_CHUNK = 256
ATTN_UNROLL = 4


def _rms(x, gain):
    return x * lax.rsqrt(jnp.mean(x * x, axis=-1, keepdims=True) + EPS) * gain


def _const_spec(shape):
    zeros = (0,) * len(shape)
    return pl.BlockSpec(shape, lambda *_: zeros, pipeline_mode=pl.Buffered(1))


def _conformer_kernel(x_ref, gmix_ref, win_ref, bin_ref, dw_ref, dwb_ref, lng_ref, lnb_ref, wout_ref, bout_ref,
                      o_ref, buf_ref, y_ref):
    tm = x_ref.shape[0]

    @pl.when(pl.program_id(0) == 0)
    def _():
        buf_ref[0:CONV_HALO, :] = jnp.zeros((CONV_HALO, D_MODEL), F32)

    x = x_ref[...]
    h = _rms(x, gmix_ref[...]).astype(BF16)
    u = jnp.dot(h, win_ref[...], preferred_element_type=F32) + bin_ref[...]
    buf_ref[CONV_HALO:CONV_HALO + tm, :] = u[:, :D_MODEL] * jax.nn.sigmoid(u[:, D_MODEL:])

    base = CONV_HALO - SUBLANES
    rows = tm + SUBLANES

    def strip(c, carry):
        lanes = pl.ds(pl.multiple_of(c * 128, 128), 128)
        y = None
        for s in range(SUBLANES):
            p = None
            for a in range(4):
                m = SUBLANES * a + s
                if m >= CONV_KERNEL:
                    continue
                w = dw_ref[CONV_KERNEL - 1 - m:CONV_KERNEL - m, lanes]
                term = buf_ref[base - SUBLANES * a:base - SUBLANES * a + rows, lanes] * w
                p = term if p is None else p + term
            shifted = p[SUBLANES - s:SUBLANES - s + tm, :]
            y = shifted if y is None else y + shifted
        y_ref[:, lanes] = y + dwb_ref[:, lanes]
        return carry

    lax.fori_loop(0, D_MODEL // 128, strip, 0)
    buf_ref[0:CONV_HALO, :] = buf_ref[tm:tm + CONV_HALO, :]

    y = y_ref[...]
    mu = jnp.mean(y, axis=-1, keepdims=True)
    yc = y - mu
    var = jnp.mean(yc * yc, axis=-1, keepdims=True)
    z = yc * lax.rsqrt(var + EPS) * lng_ref[...] + lnb_ref[...]
    z = (z * jax.nn.sigmoid(z)).astype(BF16)
    o_ref[...] = x + jnp.dot(z, wout_ref[...], preferred_element_type=F32) + bout_ref[...]


def _conformer(x, gmix, w_in, b_in, dw, dw_b, ln_g, ln_b, w_out, b_out):
    s = x.shape[0]
    tm = ROW_TILE
    row = lambda v: v.reshape(1, -1)
    return pl.pallas_call(
        _conformer_kernel,
        grid=(s // tm,),
        in_specs=[
            pl.BlockSpec((tm, D_MODEL), lambda i: (i, 0)),
            _const_spec((1, D_MODEL)),
            _const_spec((D_MODEL, 2 * D_MODEL)),
            _const_spec((1, 2 * D_MODEL)),
            _const_spec((CONV_KERNEL, D_MODEL)),
            _const_spec((1, D_MODEL)),
            _const_spec((1, D_MODEL)),
            _const_spec((1, D_MODEL)),
            _const_spec((D_MODEL, D_MODEL)),
            _const_spec((1, D_MODEL)),
        ],
        out_specs=pl.BlockSpec((tm, D_MODEL), lambda i: (i, 0)),
        out_shape=jax.ShapeDtypeStruct((s, D_MODEL), F32),
        scratch_shapes=[
            pltpu.VMEM((CONV_HALO + tm, D_MODEL), F32),
            pltpu.VMEM((tm, D_MODEL), F32),
        ],
        compiler_params=pltpu.CompilerParams(
            dimension_semantics=("arbitrary",), vmem_limit_bytes=VMEM_LIMIT),
        name="conformer",
    )(x, row(gmix), w_in.astype(BF16), row(b_in), dw, row(dw_b), row(ln_g), row(ln_b),
      w_out.astype(BF16), row(b_out))


def _ffn_kernel(*refs, with_attn):
    if with_attn:
        x_ref, a_ref, wattn_ref, *refs = refs
    else:
        x_ref, *refs = refs
    g_ref, wup_ref, dw_ref, dwb_ref, wdown_ref, o_ref, ubuf_a, ubuf_b, carry_ref, h_ref = refs
    tm = x_ref.shape[0]
    n_chunks = D_FF // FF_CHUNK

    @pl.when(pl.program_id(0) == 0)
    def _():
        carry_ref[...] = jnp.zeros(carry_ref.shape, F32)

    x = x_ref[...]
    if with_attn:
        x = x + jnp.dot(a_ref[...].astype(BF16), wattn_ref[...], preferred_element_type=F32)
    o_ref[...] = x
    h_ref[...] = _rms(x, g_ref[...]).astype(BF16)

    def gate_value(ref, rows, c):
        gate = pl.ds(pl.multiple_of(c * FF_CHUNK, FF_CHUNK), FF_CHUNK)
        value = pl.ds(pl.multiple_of(D_FF + c * FF_CHUNK, FF_CHUNK), FF_CHUNK)
        return jnp.concatenate([ref[rows, gate], ref[rows, value]], axis=1)

    def up(c, ubuf_ref):
        u = jnp.dot(h_ref[...], gate_value(wup_ref, slice(None), c), preferred_element_type=F32)
        ubuf_ref[0:FFN_HALO, :] = carry_ref[c]
        ubuf_ref[FFN_HALO:FFN_HALO + tm, :] = u
        carry_ref[c] = u[tm - FFN_HALO:tm, :]

    def down(c, ubuf_ref):
        y = gate_value(dwb_ref, slice(None), c)
        for j in range(FFN_KERNEL):
            back = FFN_KERNEL - 1 - j
            y = y + ubuf_ref[FFN_HALO - back:FFN_HALO - back + tm, :] * gate_value(dw_ref, slice(j, j + 1), c)
        gate = y[:, :FF_CHUNK]
        act = (gate * jax.nn.sigmoid(gate) * y[:, FF_CHUNK:]).astype(BF16)
        rows = pl.ds(pl.multiple_of(c * FF_CHUNK, FF_CHUNK), FF_CHUNK)
        o_ref[...] += jnp.dot(act, wdown_ref[rows, :], preferred_element_type=F32)

    up(0, ubuf_a)

    def pair(it, carry):
        c = 2 * it
        up(c + 1, ubuf_b)
        down(c, ubuf_a)
        up(c + 2, ubuf_a)
        down(c + 1, ubuf_b)
        return carry

    assert n_chunks % 2 == 1
    lax.fori_loop(0, n_chunks // 2, pair, 0)
    down(n_chunks - 1, ubuf_a)


def _ffn(x, gain, w_up, dw, dw_b, w_down, attn=None, w_attn=None):
    s = x.shape[0]
    tm = ROW_TILE
    with_attn = attn is not None
    row_spec = pl.BlockSpec((tm, D_MODEL), lambda i: (i, 0))
    args, specs = [x], [row_spec]
    if with_attn:
        args += [attn, w_attn.astype(BF16)]
        specs += [pl.BlockSpec((tm, GROUP_WIDTH), lambda i: (i, 0)), _const_spec((GROUP_WIDTH, D_MODEL))]
    args += [gain.reshape(1, -1), w_up.astype(BF16), dw, dw_b.reshape(1, -1), w_down.astype(BF16)]
    specs += [_const_spec((1, D_MODEL)), _const_spec((D_MODEL, 2 * D_FF)), _const_spec((FFN_KERNEL, 2 * D_FF)),
              _const_spec((1, 2 * D_FF)), _const_spec((D_FF, D_MODEL))]
    return pl.pallas_call(
        functools.partial(_ffn_kernel, with_attn=with_attn),
        grid=(s // tm,),
        in_specs=specs,
        out_specs=row_spec,
        out_shape=jax.ShapeDtypeStruct((s, D_MODEL), F32),
        scratch_shapes=[
            pltpu.VMEM((FFN_HALO + tm, 2 * FF_CHUNK), F32),
            pltpu.VMEM((FFN_HALO + tm, 2 * FF_CHUNK), F32),
            pltpu.VMEM((D_FF // FF_CHUNK, FFN_HALO, 2 * FF_CHUNK), F32),
            pltpu.VMEM((tm, D_MODEL), BF16),
        ],
        compiler_params=pltpu.CompilerParams(
            dimension_semantics=("arbitrary",), vmem_limit_bytes=VMEM_LIMIT),
        name="ffn_attn" if with_attn else "ffn",
    )(*args)


def _qkv_kernel(x_ref, g_ref, w_ref, gain_ref, o_ref, hp_ref, hs_ref):
    j = pl.program_id(1)
    tiles_per_group = 3 * GROUP_WIDTH // QKV_COL_TILE
    tiles_per_kind = GROUP_WIDTH // QKV_COL_TILE

    @pl.when(j == 0)
    def _():
        n_strips = D_MODEL // 128
        for c in range(ATTN_TILE // NORM_ROW_CHUNK):
            rows = slice(c * NORM_ROW_CHUNK, (c + 1) * NORM_ROW_CHUNK)
            h = _rms(x_ref[rows, :], g_ref[...])
            hp_ref[0, rows, :] = h.astype(BF16)
            for k in range(n_strips):
                hs_ref[k, rows, :] = h[:, k * 128:(k + 1) * 128]
        for g, d in enumerate(DILATIONS):
            if d == 1:
                continue
            span = BLOCK * d
            for b in range(ATTN_TILE // span):
                for r in range(d):
                    dst = slice(b * span + r * BLOCK, b * span + (r + 1) * BLOCK)
                    for k in range(n_strips):
                        src = hs_ref[k, pl.ds(b * span + r, BLOCK, stride=d), :]
                        hp_ref[g, dst, k * 128:(k + 1) * 128] = src.astype(BF16)

    grp = j // tiles_per_group
    kind = (j % tiles_per_group) // tiles_per_kind

    is_normed = kind < 2
    for c in range(ATTN_TILE // QKV_ROW_CHUNK):
        rows = slice(c * QKV_ROW_CHUNK, (c + 1) * QKV_ROW_CHUNK)
        acc = jnp.dot(hp_ref[grp, rows, :], w_ref[...], preferred_element_type=F32)
        for hh in range(QKV_COL_TILE // HEAD_DIM):
            cols = slice(hh * HEAD_DIM, (hh + 1) * HEAD_DIM)
            a = acc[:, cols]
            inv = lax.rsqrt(jnp.mean(a * a, axis=-1, keepdims=True) + EPS)
            o_ref[rows, cols] = (a * jnp.where(is_normed, inv, 1.0) * gain_ref[:, cols]).astype(BF16)


def _qkv(x, gain, w_qkv, q_norm, k_norm):
    s = x.shape[0]
    width = N_GROUPS * 3 * GROUP_WIDTH
    qn = q_norm.reshape(N_GROUPS, 1, GROUP_WIDTH) * (HEAD_DIM ** -0.5)
    kn = k_norm.reshape(N_GROUPS, 1, GROUP_WIDTH)
    head_gain = jnp.concatenate([qn, kn, jnp.ones_like(kn)], axis=1).reshape(1, width)
    return pl.pallas_call(
        _qkv_kernel,
        grid=(s // ATTN_TILE, width // QKV_COL_TILE),
        in_specs=[
            pl.BlockSpec((ATTN_TILE, D_MODEL), lambda i, j: (i, 0)),
            pl.BlockSpec((1, D_MODEL), lambda i, j: (0, 0)),
            pl.BlockSpec((D_MODEL, QKV_COL_TILE), lambda i, j: (0, j)),
            pl.BlockSpec((1, QKV_COL_TILE), lambda i, j: (0, j)),
        ],
        out_specs=pl.BlockSpec((ATTN_TILE, QKV_COL_TILE), lambda i, j: (i, j)),
        out_shape=jax.ShapeDtypeStruct((s, width), BF16),
        scratch_shapes=[
            pltpu.VMEM((N_GROUPS, ATTN_TILE, D_MODEL), BF16),
            pltpu.VMEM((D_MODEL // 128, ATTN_TILE, 128), F32),
        ],
        compiler_params=pltpu.CompilerParams(
            dimension_semantics=("arbitrary", "arbitrary"), vmem_limit_bytes=VMEM_LIMIT),
        name="qkv",
    )(x, gain.reshape(1, -1), w_qkv.astype(BF16), head_gain)


def _attn_kernel(slopes_ref, *refs):
    q_refs, kc_refs, kp_refs, vc_refs, vp_refs = (refs[3 * n:3 * n + 3] for n in range(5))
    o_ref, acc_ref, den_ref, max_ref, bias_ref = refs[15:]
    tile = pl.program_id(0)
    head = pl.program_id(1)
    n_blocks = ATTN_TILE // BLOCK
    quad = 4 * BLOCK

    qi = lax.broadcasted_iota(jnp.int32, (BLOCK, 2 * BLOCK), 0)
    ki = lax.broadcasted_iota(jnp.int32, (BLOCK, 2 * BLOCK), 1)
    delta = qi + BLOCK - ki
    in_window = (delta >= 0) & (delta <= N_BACK)
    has_prev = in_window & ((ki >= BLOCK) | (tile > 0))
    for g, d in enumerate(DILATIONS):
        bias = -slopes_ref[g, head] * (delta * d).astype(F32)
        bias_ref[g, 0] = jnp.where(in_window, bias, 1.0)
        bias_ref[g, 1] = jnp.where(has_prev, bias, 1.0)

    ones = jnp.ones((2 * BLOCK, HEAD_DIM), BF16)

    def store(ref, g, n, val):
        if DILATIONS[g] == 16:
            for b in range(ATTN_TILE // quad):
                dst = pl.ds(quad * b + BLOCK * (n % 4) + n // 4, BLOCK // 4, stride=4)
                ref[g, dst, :] = val[b * (BLOCK // 4):(b + 1) * (BLOCK // 4), :]
        else:
            ref[g, n * BLOCK:(n + 1) * BLOCK, :] = val

    def scores(g, n):
        d = DILATIONS[g]
        rows = slice(n * BLOCK, (n + 1) * BLOCK)
        if n < d:
            prev_k, prev_v = kp_refs[g][rows, :], vp_refs[g][rows, :]
        else:
            back = slice((n - d) * BLOCK, (n - d + 1) * BLOCK)
            prev_k, prev_v = kc_refs[g][back, :], vc_refs[g][back, :]
        k2 = jnp.concatenate([prev_k, kc_refs[g][rows, :]], axis=0)
        sc = lax.dot_general(q_refs[g][rows, :], k2, (((1,), (1,)), ((), ())), preferred_element_type=F32)
        bias = bias_ref[g, 1 if n < d else 0]
        sc = jnp.where(bias <= 0.0, sc + bias, NEG)
        m = jnp.max(sc, axis=-1, keepdims=True)
        store(max_ref, g, n, jnp.broadcast_to(m, (BLOCK, HEAD_DIM)))
        return jnp.exp(sc - m).astype(BF16), prev_v

    def values(g, n, p, prev_v):
        v2 = jnp.concatenate([prev_v, vc_refs[g][n * BLOCK:(n + 1) * BLOCK, :]], axis=0)
        od = jnp.dot(p, jnp.concatenate([v2, ones], axis=1), preferred_element_type=F32)
        store(acc_ref, g, n, od[:, :HEAD_DIM])
        store(den_ref, g, n, od[:, HEAD_DIM:])

    blocks = [(g, n) for g in range(N_GROUPS) for n in range(n_blocks)]
    items = [blocks[i:i + ATTN_UNROLL] for i in range(0, len(blocks), ATTN_UNROLL)]
    pending = [scores(g, n) for g, n in items[0]]
    for i, item in enumerate(items):
        following = [scores(g, n) for g, n in items[i + 1]] if i + 1 < len(items) else []
        for (g, n), (p, prev_v) in zip(item, pending):
            values(g, n, p, prev_v)
        pending = following

    def merge(c, carry):
        seg = lax.shift_right_logical(c, 2) * quad + (c & 3)
        in_position_order = pl.ds(seg, BLOCK, stride=4)
        rows = pl.ds(pl.multiple_of(c * BLOCK, BLOCK), BLOCK)
        pick = lambda ref, g: ref[g, in_position_order if DILATIONS[g] == 1 else rows, :]
        tops = [pick(max_ref, g) for g in range(N_GROUPS)]
        top = functools.reduce(jnp.maximum, tops)
        e = [jnp.exp(t - top) for t in tops]
        num = sum(e[g] * pick(acc_ref, g) for g in range(N_GROUPS))
        den = sum(e[g] * pick(den_ref, g) for g in range(N_GROUPS))
        o_ref[in_position_order, :] = num / den
        return carry

    lax.fori_loop(0, n_blocks, merge, 0)


def _attention(qkv):
    s = qkv.shape[0]
    slopes = jnp.asarray(
        2.0 ** (-ALIBI_MAX * (np.arange(N_GROUPS * HEADS, dtype=np.float32) + 1.0) / (N_GROUPS * HEADS)),
        dtype=F32).reshape(N_GROUPS, HEADS)

    def col(g, kind):
        return lambda t, h: (t, (3 * g + kind) * HEADS + h)

    def prev_col(g, kind):
        ratio = ATTN_TILE // (BLOCK * DILATIONS[g])
        return lambda t, h: (jnp.maximum(t * ratio - 1, 0), (3 * g + kind) * HEADS + h)

    cur = lambda kind: [pl.BlockSpec((ATTN_TILE, HEAD_DIM), col(g, kind)) for g in range(N_GROUPS)]
    prev = lambda kind: [pl.BlockSpec((BLOCK * DILATIONS[g], HEAD_DIM), prev_col(g, kind)) for g in range(N_GROUPS)]
    in_specs = ([pl.BlockSpec(memory_space=pltpu.SMEM)] + cur(0) + cur(1) + prev(1) + cur(2) + prev(2))
    return pl.pallas_call(
        _attn_kernel,
        grid=(s // ATTN_TILE, HEADS),
        in_specs=in_specs,
        out_specs=pl.BlockSpec((ATTN_TILE, HEAD_DIM), lambda t, h: (t, h)),
        out_shape=jax.ShapeDtypeStruct((s, GROUP_WIDTH), F32),
        scratch_shapes=[
            pltpu.VMEM((N_GROUPS, ATTN_TILE, HEAD_DIM), F32),
            pltpu.VMEM((N_GROUPS, ATTN_TILE, HEAD_DIM), F32),
            pltpu.VMEM((N_GROUPS, ATTN_TILE, HEAD_DIM), F32),
            pltpu.VMEM((N_GROUPS, 2, BLOCK, 2 * BLOCK), F32),
        ],
        compiler_params=pltpu.CompilerParams(
            dimension_semantics=("arbitrary", "arbitrary"), vmem_limit_bytes=VMEM_LIMIT),
        name="dilated_attention",
    )(slopes, *([qkv] * 15))


def kernel(x, norm_mix, norm_ffn, cm_w_in, cm_b_in, cm_dw, cm_dw_b, cm_ln_g, cm_ln_b, cm_w_out, cm_b_out,
           at_w_qkv, at_q_norm, at_k_norm, at_w_out, ff_w_up, ff_dw, ff_dw_b, ff_w_down):
    batch, seq, _ = x.shape
    assert seq % ATTN_TILE == 0 and x.shape[-1] == D_MODEL
    outs = []
    for b in range(batch):
        h = x[b]
        h = _conformer(h, norm_mix[0], cm_w_in[0], cm_b_in[0], cm_dw[0], cm_dw_b[0], cm_ln_g[0], cm_ln_b[0],
                       cm_w_out[0], cm_b_out[0])
        h = _ffn(h, norm_ffn[0], ff_w_up[0], ff_dw[0], ff_dw_b[0], ff_w_down[0])
        qkv = _qkv(h, norm_mix[1], at_w_qkv[0], at_q_norm[0], at_k_norm[0])
        attn = _attention(qkv)
        h = _ffn(h, norm_ffn[1], ff_w_up[1], ff_dw[1], ff_dw_b[1], ff_w_down[1], attn=attn, w_attn=at_w_out[0])
        outs.append(h)
    return jnp.stack(outs, axis=0)
```

```python
import functools

import numpy as np
import jax
import jax.numpy as jnp
from jax import lax
from jax.experimental import pallas as pl
from jax.experimental.pallas import tpu as pltpu

F32 = jnp.float32
BF16 = jnp.bfloat16

D_MODEL = 1024
HEAD_DIM = 128
HEADS = 8
GROUP_WIDTH = HEADS * HEAD_DIM
DILATIONS = (1, 4, 16)
N_GROUPS = len(DILATIONS)
N_BACK = 128
BLOCK = 128
ATTN_TILE = BLOCK * DILATIONS[-1]
QUAD = 4 * BLOCK
ALIBI_MAX = 8.0
CONV_KERNEL = 31
CONV_HALO = 32
FFN_KERNEL = 3
FFN_HALO = 8
D_FF = 2816
FF_CHUNK = 256
EPS = 1e-6
NEG = -1e30
LANES = 128
SUBLANES = 8
VMEM_LIMIT = 60 * 1024 * 1024

CONF_ROW_TILE = 1024
CONF_SUB = 256
CONV_ROWS = 64
FFN_ROW_TILE = 512
QKV_COL_TILE = 1024
QKV_ROW_CHUNK = 512
NORM_ROW_CHUNK = 256
ATTN_UNROLL = 4


def _rms(x, gain):
    return x * lax.rsqrt(jnp.mean(x * x, axis=-1, keepdims=True) + EPS) * gain


def _spaced(start, size):
    return pl.ds(2 * start, size, stride=2)


def _const_spec(shape):
    zeros = (0,) * len(shape)
    return pl.BlockSpec(shape, lambda *_: zeros, pipeline_mode=pl.Buffered(1))


def _conformer_kernel(x_ref, gmix_ref, win_ref, bin_ref, dw_ref, dwb_ref, lng_ref, lnb_ref, wout_ref, bout_ref,
                      o_ref, buf_ref, y_ref):
    tm = x_ref.shape[0]
    n_sub = tm // CONF_SUB

    @pl.when(pl.program_id(0) == 0)
    def _():
        for c in range(D_MODEL // LANES):
            buf_ref[c, _spaced(0, CONV_HALO), :] = jnp.zeros((CONV_HALO, LANES), F32)

    def glu(s):
        first = pl.multiple_of(s * CONF_SUB, CONF_SUB)
        h = _rms(x_ref[pl.ds(first, CONF_SUB), :], gmix_ref[...]).astype(BF16)
        u = jnp.dot(h, win_ref[...], preferred_element_type=F32) + bin_ref[...]
        g = u[:, :D_MODEL] * jax.nn.sigmoid(u[:, D_MODEL:])
        for c in range(D_MODEL // LANES):
            buf_ref[c, _spaced(CONV_HALO + first, CONF_SUB), :] = g[:, c * LANES:(c + 1) * LANES]

    def conv(s):
        for c in range(D_MODEL // LANES):
            lanes = slice(c * LANES, (c + 1) * LANES)
            for q in range(CONF_SUB // CONV_ROWS):
                first = pl.multiple_of(s * CONF_SUB + q * CONV_ROWS, CONV_ROWS)
                y = dwb_ref[:, lanes]
                for m in range(CONV_KERNEL):
                    w = dw_ref[CONV_KERNEL - 1 - m:CONV_KERNEL - m, lanes]
                    y = y + buf_ref[c, _spaced(CONV_HALO + first - m, CONV_ROWS), :] * w
                y_ref[pl.ds(first, CONV_ROWS), lanes] = y

    def project(s):
        rows = pl.ds(pl.multiple_of(s * CONF_SUB, CONF_SUB), CONF_SUB)
        y = y_ref[rows, :]
        mu = jnp.mean(y, axis=-1, keepdims=True)
        yc = y - mu
        var = jnp.mean(yc * yc, axis=-1, keepdims=True)
        z = yc * lax.rsqrt(var + EPS) * lng_ref[...] + lnb_ref[...]
        z = (z * jax.nn.sigmoid(z)).astype(BF16)
        o_ref[rows, :] = x_ref[rows, :] + jnp.dot(z, wout_ref[...], preferred_element_type=F32) + bout_ref[...]

    glu(0)

    def body(s, carry):
        conv(s)
        glu(s + 1)
        project(s)
        return carry

    lax.fori_loop(0, n_sub - 1, body, 0)
    conv(n_sub - 1)
    project(n_sub - 1)
    for c in range(D_MODEL // LANES):
        buf_ref[c, _spaced(0, CONV_HALO), :] = buf_ref[c, _spaced(tm, CONV_HALO), :]


def _conformer(x, gmix, w_in, b_in, dw, dw_b, ln_g, ln_b, w_out, b_out):
    s = x.shape[0]
    tm = CONF_ROW_TILE
    row = lambda v: v.reshape(1, -1)
    return pl.pallas_call(
        _conformer_kernel,
        grid=(s // tm,),
        in_specs=[
            pl.BlockSpec((tm, D_MODEL), lambda i: (i, 0)),
            _const_spec((1, D_MODEL)),
            _const_spec((D_MODEL, 2 * D_MODEL)),
            _const_spec((1, 2 * D_MODEL)),
            _const_spec((CONV_KERNEL, D_MODEL)),
            _const_spec((1, D_MODEL)),
            _const_spec((1, D_MODEL)),
            _const_spec((1, D_MODEL)),
            _const_spec((D_MODEL, D_MODEL)),
            _const_spec((1, D_MODEL)),
        ],
        out_specs=pl.BlockSpec((tm, D_MODEL), lambda i: (i, 0)),
        out_shape=jax.ShapeDtypeStruct((s, D_MODEL), F32),
        scratch_shapes=[
            pltpu.VMEM((D_MODEL // LANES, 2 * (CONV_HALO + tm), LANES), F32),
            pltpu.VMEM((tm, D_MODEL), F32),
        ],
        compiler_params=pltpu.CompilerParams(
            dimension_semantics=("arbitrary",), vmem_limit_bytes=VMEM_LIMIT),
        name="conformer",
    )(x, row(gmix), w_in.astype(BF16), row(b_in), dw, row(dw_b), row(ln_g), row(ln_b),
      w_out.astype(BF16), row(b_out))


def _ffn_kernel(*refs, with_attn):
    if with_attn:
        x_ref, a_ref, wattn_ref, *refs = refs
    else:
        x_ref, *refs = refs
    g_ref, wup_ref, dw_ref, dwb_ref, wdown_ref, o_ref, ubuf_a, ubuf_b, carry_ref, h_ref = refs
    tm = x_ref.shape[0]
    n_chunks = D_FF // FF_CHUNK

    @pl.when(pl.program_id(0) == 0)
    def _():
        carry_ref[...] = jnp.zeros(carry_ref.shape, F32)

    x = x_ref[...]
    if with_attn:
        x = x + jnp.dot(a_ref[...].astype(BF16), wattn_ref[...], preferred_element_type=F32)
    o_ref[...] = x
    h_ref[...] = _rms(x, g_ref[...]).astype(BF16)

    def gate_value(ref, rows, c):
        gate = pl.ds(pl.multiple_of(c * FF_CHUNK, FF_CHUNK), FF_CHUNK)
        value = pl.ds(pl.multiple_of(D_FF + c * FF_CHUNK, FF_CHUNK), FF_CHUNK)
        return jnp.concatenate([ref[rows, gate], ref[rows, value]], axis=1)

    n_strips = 2 * FF_CHUNK // LANES

    def up(c, ubuf_ref):
        u = jnp.dot(h_ref[...], gate_value(wup_ref, slice(None), c), preferred_element_type=F32)
        for k in range(n_strips):
            strip = u[:, k * LANES:(k + 1) * LANES]
            ubuf_ref[k, _spaced(0, FFN_HALO), :] = carry_ref[c, k]
            ubuf_ref[k, _spaced(FFN_HALO, tm), :] = strip
            carry_ref[c, k] = strip[tm - FFN_HALO:tm, :]

    def down(c, ubuf_ref):
        bias = gate_value(dwb_ref, slice(None), c)
        taps = [gate_value(dw_ref, slice(j, j + 1), c) for j in range(FFN_KERNEL)]
        strips = []
        for k in range(n_strips):
            lanes = slice(k * LANES, (k + 1) * LANES)
            y = bias[:, lanes]
            for j in range(FFN_KERNEL):
                y = y + ubuf_ref[k, _spaced(FFN_HALO - (FFN_KERNEL - 1 - j), tm), :] * taps[j][:, lanes]
            strips.append(y)
        y = jnp.concatenate(strips, axis=1)
        gate = y[:, :FF_CHUNK]
        act = (gate * jax.nn.sigmoid(gate) * y[:, FF_CHUNK:]).astype(BF16)
        rows = pl.ds(pl.multiple_of(c * FF_CHUNK, FF_CHUNK), FF_CHUNK)
        o_ref[...] += jnp.dot(act, wdown_ref[rows, :], preferred_element_type=F32)

    up(0, ubuf_a)

    def pair(it, carry):
        c = 2 * it
        up(c + 1, ubuf_b)
        down(c, ubuf_a)
        up(c + 2, ubuf_a)
        down(c + 1, ubuf_b)
        return carry

    assert n_chunks % 2 == 1
    lax.fori_loop(0, n_chunks // 2, pair, 0)
    down(n_chunks - 1, ubuf_a)


def _ffn(x, gain, w_up, dw, dw_b, w_down, attn=None, w_attn=None):
    s = x.shape[0]
    tm = FFN_ROW_TILE
    with_attn = attn is not None
    row_spec = pl.BlockSpec((tm, D_MODEL), lambda i: (i, 0))
    args, specs = [x], [row_spec]
    if with_attn:
        args += [attn, w_attn.astype(BF16)]
        specs += [pl.BlockSpec((tm, GROUP_WIDTH), lambda i: (i, 0)), _const_spec((GROUP_WIDTH, D_MODEL))]
    args += [gain.reshape(1, -1), w_up.astype(BF16), dw, dw_b.reshape(1, -1), w_down.astype(BF16)]
    specs += [_const_spec((1, D_MODEL)), _const_spec((D_MODEL, 2 * D_FF)), _const_spec((FFN_KERNEL, 2 * D_FF)),
              _const_spec((1, 2 * D_FF)), _const_spec((D_FF, D_MODEL))]
    return pl.pallas_call(
        functools.partial(_ffn_kernel, with_attn=with_attn),
        grid=(s // tm,),
        in_specs=specs,
        out_specs=row_spec,
        out_shape=jax.ShapeDtypeStruct((s, D_MODEL), F32),
        scratch_shapes=[
            pltpu.VMEM((2 * FF_CHUNK // LANES, 2 * (FFN_HALO + tm), LANES), F32),
            pltpu.VMEM((2 * FF_CHUNK // LANES, 2 * (FFN_HALO + tm), LANES), F32),
            pltpu.VMEM((D_FF // FF_CHUNK, 2 * FF_CHUNK // LANES, FFN_HALO, LANES), F32),
            pltpu.VMEM((tm, D_MODEL), BF16),
        ],
        compiler_params=pltpu.CompilerParams(
            dimension_semantics=("arbitrary",), vmem_limit_bytes=VMEM_LIMIT),
        name="ffn_attn" if with_attn else "ffn",
    )(*args)


def _qkv_kernel(x_ref, g_ref, w_ref, gain_ref, o_ref, hp_ref, hs_ref, quad_ref):
    j = pl.program_id(1)
    tiles_per_group = 3 * GROUP_WIDTH // QKV_COL_TILE
    tiles_per_kind = GROUP_WIDTH // QKV_COL_TILE

    @pl.when(j == 0)
    def _():
        n_strips = D_MODEL // LANES
        for c in range(ATTN_TILE // NORM_ROW_CHUNK):
            rows = slice(c * NORM_ROW_CHUNK, (c + 1) * NORM_ROW_CHUNK)
            h = _rms(x_ref[rows, :], g_ref[...])
            hp_ref[0, rows, :] = h.astype(BF16)
            for k in range(n_strips):
                hs_ref[k, rows, :] = h[:, k * LANES:(k + 1) * LANES]
        for k in range(n_strips):
            lanes = slice(k * LANES, (k + 1) * LANES)
            quad = quad_ref.at[k % 2]
            for b in range(ATTN_TILE // QUAD):
                for r in range(4):
                    dst = slice(b * QUAD + r * BLOCK, b * QUAD + (r + 1) * BLOCK)
                    v = hs_ref[k, pl.ds(b * QUAD + r, BLOCK, stride=4), :]
                    hp_ref[1, dst, lanes] = v.astype(BF16)
                    quad[dst, :] = v
            for r in range(DILATIONS[2]):
                for b in range(ATTN_TILE // QUAD):
                    src = pl.ds(b * QUAD + (r % 4) * BLOCK + r // 4, BLOCK // 4, stride=4)
                    dst = slice(r * BLOCK + b * (BLOCK // 4), r * BLOCK + (b + 1) * (BLOCK // 4))
                    hp_ref[2, dst, lanes] = quad[src, :].astype(BF16)

    grp = j // tiles_per_group
    kind = (j % tiles_per_group) // tiles_per_kind

    is_normed = kind < 2
    for c in range(ATTN_TILE // QKV_ROW_CHUNK):
        rows = slice(c * QKV_ROW_CHUNK, (c + 1) * QKV_ROW_CHUNK)
        acc = jnp.dot(hp_ref[grp, rows, :], w_ref[...], preferred_element_type=F32)
        for hh in range(QKV_COL_TILE // HEAD_DIM):
            cols = slice(hh * HEAD_DIM, (hh + 1) * HEAD_DIM)
            a = acc[:, cols]
            inv = lax.rsqrt(jnp.mean(a * a, axis=-1, keepdims=True) + EPS)
            o_ref[rows, cols] = (a * jnp.where(is_normed, inv, 1.0) * gain_ref[:, cols]).astype(BF16)


def _qkv(x, gain, w_qkv, q_norm, k_norm):
    s = x.shape[0]
    width = N_GROUPS * 3 * GROUP_WIDTH
    qn = q_norm.reshape(N_GROUPS, 1, GROUP_WIDTH) * (HEAD_DIM ** -0.5)
    kn = k_norm.reshape(N_GROUPS, 1, GROUP_WIDTH)
    head_gain = jnp.concatenate([qn, kn, jnp.ones_like(kn)], axis=1).reshape(1, width)
    return pl.pallas_call(
        _qkv_kernel,
        grid=(s // ATTN_TILE, width // QKV_COL_TILE),
        in_specs=[
            pl.BlockSpec((ATTN_TILE, D_MODEL), lambda i, j: (i, 0)),
            pl.BlockSpec((1, D_MODEL), lambda i, j: (0, 0)),
            pl.BlockSpec((D_MODEL, QKV_COL_TILE), lambda i, j: (0, j)),
            pl.BlockSpec((1, QKV_COL_TILE), lambda i, j: (0, j)),
        ],
        out_specs=pl.BlockSpec((ATTN_TILE, QKV_COL_TILE), lambda i, j: (i, j)),
        out_shape=jax.ShapeDtypeStruct((s, width), BF16),
        scratch_shapes=[
            pltpu.VMEM((N_GROUPS, ATTN_TILE, D_MODEL), BF16),
            pltpu.VMEM((D_MODEL // LANES, ATTN_TILE, LANES), F32),
            pltpu.VMEM((2, ATTN_TILE, LANES), F32),
        ],
        compiler_params=pltpu.CompilerParams(
            dimension_semantics=("arbitrary", "arbitrary"), vmem_limit_bytes=VMEM_LIMIT),
        name="qkv",
    )(x, gain.reshape(1, -1), w_qkv.astype(BF16), head_gain)


def _attn_kernel(slopes_ref, *refs):
    q_refs, kc_refs, kp_refs, vc_refs, vp_refs = (refs[3 * n:3 * n + 3] for n in range(5))
    o_ref, acc_ref, den_ref, max_ref, bias_ref = refs[15:]
    tile = pl.program_id(0)
    head = pl.program_id(1)
    n_blocks = ATTN_TILE // BLOCK

    qi = lax.broadcasted_iota(jnp.int32, (BLOCK, 2 * BLOCK), 0)
    ki = lax.broadcasted_iota(jnp.int32, (BLOCK, 2 * BLOCK), 1)
    delta = qi + BLOCK - ki
    in_window = (delta >= 0) & (delta <= N_BACK)
    has_prev = in_window & ((ki >= BLOCK) | (tile > 0))
    for g, d in enumerate(DILATIONS):
        bias = -slopes_ref[g, head] * (delta * d).astype(F32)
        bias_ref[g, 0] = jnp.where(in_window, bias, 1.0)
        bias_ref[g, 1] = jnp.where(has_prev, bias, 1.0)

    ones = jnp.ones((2 * BLOCK, HEAD_DIM), BF16)

    def store(ref, g, n, val):
        if DILATIONS[g] == 16:
            for b in range(ATTN_TILE // QUAD):
                dst = pl.ds(QUAD * b + BLOCK * (n % 4) + n // 4, BLOCK // 4, stride=4)
                ref[g, dst, :] = val[b * (BLOCK // 4):(b + 1) * (BLOCK // 4), :]
        else:
            ref[g, n * BLOCK:(n + 1) * BLOCK, :] = val

    def scores(g, n):
        d = DILATIONS[g]
        rows = slice(n * BLOCK, (n + 1) * BLOCK)
        if n < d:
            prev_k, prev_v = kp_refs[g][rows, :], vp_refs[g][rows, :]
        else:
            back = slice((n - d) * BLOCK, (n - d + 1) * BLOCK)
            prev_k, prev_v = kc_refs[g][back, :], vc_refs[g][back, :]
        k2 = jnp.concatenate([prev_k, kc_refs[g][rows, :]], axis=0)
        sc = lax.dot_general(q_refs[g][rows, :], k2, (((1,), (1,)), ((), ())), preferred_element_type=F32)
        bias = bias_ref[g, 1 if n < d else 0]
        sc = jnp.where(bias <= 0.0, sc + bias, NEG)
        m = jnp.max(sc, axis=-1, keepdims=True)
        store(max_ref, g, n, jnp.broadcast_to(m, (BLOCK, HEAD_DIM)))
        return jnp.exp(sc - m).astype(BF16), prev_v

    def values(g, n, p, prev_v):
        v2 = jnp.concatenate([prev_v, vc_refs[g][n * BLOCK:(n + 1) * BLOCK, :]], axis=0)
        od = jnp.dot(p, jnp.concatenate([v2, ones], axis=1), preferred_element_type=F32)
        store(acc_ref, g, n, od[:, :HEAD_DIM])
        store(den_ref, g, n, od[:, HEAD_DIM:])

    blocks = [(g, n) for g in range(N_GROUPS) for n in range(n_blocks)]
    items = [blocks[i:i + ATTN_UNROLL] for i in range(0, len(blocks), ATTN_UNROLL)]
    pending = [scores(g, n) for g, n in items[0]]
    for i, item in enumerate(items):
        following = [scores(g, n) for g, n in items[i + 1]] if i + 1 < len(items) else []
        for (g, n), (p, prev_v) in zip(item, pending):
            values(g, n, p, prev_v)
        pending = following

    def merge(c, carry):
        seg = lax.shift_right_logical(c, 2) * QUAD + (c & 3)
        in_position_order = pl.ds(seg, BLOCK, stride=4)
        rows = pl.ds(pl.multiple_of(c * BLOCK, BLOCK), BLOCK)
        pick = lambda ref, g: ref[g, in_position_order if DILATIONS[g] == 1 else rows, :]
        tops = [pick(max_ref, g) for g in range(N_GROUPS)]
        top = functools.reduce(jnp.maximum, tops)
        e = [jnp.exp(t - top) for t in tops]
        num = sum(e[g] * pick(acc_ref, g) for g in range(N_GROUPS))
        den = sum(e[g] * pick(den_ref, g) for g in range(N_GROUPS))
        o_ref[in_position_order, :] = num / den
        return carry

    lax.fori_loop(0, n_blocks, merge, 0)


def _attention(qkv):
    s = qkv.shape[0]
    slopes = jnp.asarray(
        2.0 ** (-ALIBI_MAX * (np.arange(N_GROUPS * HEADS, dtype=np.float32) + 1.0) / (N_GROUPS * HEADS)),
        dtype=F32).reshape(N_GROUPS, HEADS)

    def col(g, kind):
        return lambda t, h: (t, (3 * g + kind) * HEADS + h)

    def prev_col(g, kind):
        ratio = ATTN_TILE // (BLOCK * DILATIONS[g])
        return lambda t, h: (jnp.maximum(t * ratio - 1, 0), (3 * g + kind) * HEADS + h)

    cur = lambda kind: [pl.BlockSpec((ATTN_TILE, HEAD_DIM), col(g, kind)) for g in range(N_GROUPS)]
    prev = lambda kind: [pl.BlockSpec((BLOCK * DILATIONS[g], HEAD_DIM), prev_col(g, kind)) for g in range(N_GROUPS)]
    in_specs = ([pl.BlockSpec(memory_space=pltpu.SMEM)] + cur(0) + cur(1) + prev(1) + cur(2) + prev(2))
    return pl.pallas_call(
        _attn_kernel,
        grid=(s // ATTN_TILE, HEADS),
        in_specs=in_specs,
        out_specs=pl.BlockSpec((ATTN_TILE, HEAD_DIM), lambda t, h: (t, h)),
        out_shape=jax.ShapeDtypeStruct((s, GROUP_WIDTH), F32),
        scratch_shapes=[
            pltpu.VMEM((N_GROUPS, ATTN_TILE, HEAD_DIM), F32),
            pltpu.VMEM((N_GROUPS, ATTN_TILE, HEAD_DIM), F32),
            pltpu.VMEM((N_GROUPS, ATTN_TILE, HEAD_DIM), F32),
            pltpu.VMEM((N_GROUPS, 2, BLOCK, 2 * BLOCK), F32),
        ],
        compiler_params=pltpu.CompilerParams(
            dimension_semantics=("arbitrary", "arbitrary"), vmem_limit_bytes=VMEM_LIMIT),
        name="dilated_attention",
    )(slopes, *([qkv] * 15))


def kernel(x, norm_mix, norm_ffn, cm_w_in, cm_b_in, cm_dw, cm_dw_b, cm_ln_g, cm_ln_b, cm_w_out, cm_b_out,
           at_w_qkv, at_q_norm, at_k_norm, at_w_out, ff_w_up, ff_dw, ff_dw_b, ff_w_down):
    batch, seq, _ = x.shape
    assert seq % ATTN_TILE == 0 and x.shape[-1] == D_MODEL
    outs = []
    for b in range(batch):
        h = x[b]
        h = _conformer(h, norm_mix[0], cm_w_in[0], cm_b_in[0], cm_dw[0], cm_dw_b[0], cm_ln_g[0], cm_ln_b[0],
                       cm_w_out[0], cm_b_out[0])
        h = _ffn(h, norm_ffn[0], ff_w_up[0], ff_dw[0], ff_dw_b[0], ff_w_down[0])
        qkv = _qkv(h, norm_mix[1], at_w_qkv[0], at_q_norm[0], at_k_norm[0])
        attn = _attention(qkv)
        h = _ffn(h, norm_ffn[1], ff_w_up[1], ff_dw[1], ff_dw_b[1], ff_w_down[1], attn=attn, w_attn=at_w_out[0])
        outs.append(h)
    return jnp.stack(outs, axis=0)
```

```python
import functools

import numpy as np
import jax
import jax.numpy as jnp
from jax import lax
from jax.experimental import pallas as pl
from jax.experimental.pallas import tpu as pltpu

F32 = jnp.float32
BF16 = jnp.bfloat16

D_MODEL = 1024
HEAD_DIM = 128
HEADS = 8
GROUP_WIDTH = HEADS * HEAD_DIM
DILATIONS = (1, 4, 16)
N_GROUPS = len(DILATIONS)
N_BACK = 128
BLOCK = 128
ATTN_TILE = BLOCK * DILATIONS[-1]
QUAD = 4 * BLOCK
ALIBI_MAX = 8.0
CONV_KERNEL = 31
CONV_HALO = 32
FFN_KERNEL = 3
FFN_HALO = 8
D_FF = 2816
FF_CHUNK = 256
EPS = 1e-6
NEG = -1e30
LANES = 128
SUBLANES = 8
VMEM_LIMIT = 60 * 1024 * 1024

CONF_ROW_TILE = 1024
CONF_SUB = 256
CONV_ROWS = 64
FFN_ROW_TILE = 1024
QKV_COL_TILE = 1024
QKV_ROW_CHUNK = 512
NORM_ROW_CHUNK = 256
ATTN_UNROLL = 4


def _rms(x, gain):
    return x * lax.rsqrt(jnp.mean(x * x, axis=-1, keepdims=True) + EPS) * gain


def _spaced(start, size):
    return pl.ds(2 * start, size, stride=2)


def _layer_spec(layer, shape):
    index = (layer,) + (0,) * len(shape)
    return pl.BlockSpec((None, *shape), lambda *_: index, pipeline_mode=pl.Buffered(1))


def _conformer_kernel(x_ref, gmix_ref, win_ref, bin_ref, dw_ref, dwb_ref, lng_ref, lnb_ref, wout_ref, bout_ref,
                      o_ref, buf_ref, y_ref):
    tm = x_ref.shape[0]
    n_sub = tm // CONF_SUB

    @pl.when(pl.program_id(0) == 0)
    def _():
        for c in range(D_MODEL // LANES):
            buf_ref[c, _spaced(0, CONV_HALO), :] = jnp.zeros((CONV_HALO, LANES), F32)

    def glu(s):
        first = pl.multiple_of(s * CONF_SUB, CONF_SUB)
        h = _rms(x_ref[pl.ds(first, CONF_SUB), :], gmix_ref[...]).astype(BF16)
        u = jnp.dot(h, win_ref[...], preferred_element_type=F32) + bin_ref[...]
        g = u[:, :D_MODEL] * jax.nn.sigmoid(u[:, D_MODEL:])
        for c in range(D_MODEL // LANES):
            buf_ref[c, _spaced(CONV_HALO + first, CONF_SUB), :] = g[:, c * LANES:(c + 1) * LANES]

    def conv(s):
        for c in range(D_MODEL // LANES):
            lanes = slice(c * LANES, (c + 1) * LANES)
            for q in range(CONF_SUB // CONV_ROWS):
                first = pl.multiple_of(s * CONF_SUB + q * CONV_ROWS, CONV_ROWS)
                y = dwb_ref[:, lanes]
                for m in range(CONV_KERNEL):
                    w = dw_ref[CONV_KERNEL - 1 - m:CONV_KERNEL - m, lanes]
                    y = y + buf_ref[c, _spaced(CONV_HALO + first - m, CONV_ROWS), :] * w
                y_ref[pl.ds(first, CONV_ROWS), lanes] = y

    def project(s):
        rows = pl.ds(pl.multiple_of(s * CONF_SUB, CONF_SUB), CONF_SUB)
        y = y_ref[rows, :]
        mu = jnp.mean(y, axis=-1, keepdims=True)
        yc = y - mu
        var = jnp.mean(yc * yc, axis=-1, keepdims=True)
        z = yc * lax.rsqrt(var + EPS) * lng_ref[...] + lnb_ref[...]
        z = (z * jax.nn.sigmoid(z)).astype(BF16)
        o_ref[rows, :] = x_ref[rows, :] + jnp.dot(z, wout_ref[...], preferred_element_type=F32) + bout_ref[...]

    glu(0)

    def body(s, carry):
        conv(s)
        glu(s + 1)
        project(s)
        return carry

    lax.fori_loop(0, n_sub - 1, body, 0)
    conv(n_sub - 1)
    project(n_sub - 1)
    for c in range(D_MODEL // LANES):
        buf_ref[c, _spaced(0, CONV_HALO), :] = buf_ref[c, _spaced(tm, CONV_HALO), :]


def _conformer(x, layer, conv_layer, gmix, w_in, b_in, dw, dw_b, ln_g, ln_b, w_out, b_out):
    s = x.shape[0]
    tm = CONF_ROW_TILE
    return pl.pallas_call(
        _conformer_kernel,
        grid=(s // tm,),
        in_specs=[
            pl.BlockSpec((tm, D_MODEL), lambda i: (i, 0)),
            _layer_spec(layer, (1, D_MODEL)),
            _layer_spec(conv_layer, (D_MODEL, 2 * D_MODEL)),
            _layer_spec(conv_layer, (1, 2 * D_MODEL)),
            _layer_spec(conv_layer, (CONV_KERNEL, D_MODEL)),
            _layer_spec(conv_layer, (1, D_MODEL)),
            _layer_spec(conv_layer, (1, D_MODEL)),
            _layer_spec(conv_layer, (1, D_MODEL)),
            _layer_spec(conv_layer, (D_MODEL, D_MODEL)),
            _layer_spec(conv_layer, (1, D_MODEL)),
        ],
        out_specs=pl.BlockSpec((tm, D_MODEL), lambda i: (i, 0)),
        out_shape=jax.ShapeDtypeStruct((s, D_MODEL), F32),
        scratch_shapes=[
            pltpu.VMEM((D_MODEL // LANES, 2 * (CONV_HALO + tm), LANES), F32),
            pltpu.VMEM((tm, D_MODEL), F32),
        ],
        compiler_params=pltpu.CompilerParams(
            dimension_semantics=("arbitrary",), vmem_limit_bytes=VMEM_LIMIT),
        name="conformer",
    )(x, gmix, w_in, b_in, dw, dw_b, ln_g, ln_b, w_out, b_out)


def _ffn_kernel(*refs, with_attn):
    if with_attn:
        x_ref, a_ref, wattn_ref, *refs = refs
    else:
        x_ref, *refs = refs
    g_ref, wup_ref, dw_ref, dwb_ref, wdown_ref, o_ref, ubuf_a, ubuf_b, carry_ref, h_ref = refs
    tm = x_ref.shape[0]
    n_chunks = D_FF // FF_CHUNK

    @pl.when(pl.program_id(0) == 0)
    def _():
        carry_ref[...] = jnp.zeros(carry_ref.shape, F32)

    x = x_ref[...]
    if with_attn:
        x = x + jnp.dot(a_ref[...].astype(BF16), wattn_ref[...], preferred_element_type=F32)
    o_ref[...] = x
    h_ref[...] = _rms(x, g_ref[...]).astype(BF16)

    def gate_value(ref, rows, c):
        gate = pl.ds(pl.multiple_of(c * FF_CHUNK, FF_CHUNK), FF_CHUNK)
        value = pl.ds(pl.multiple_of(D_FF + c * FF_CHUNK, FF_CHUNK), FF_CHUNK)
        return jnp.concatenate([ref[rows, gate], ref[rows, value]], axis=1)

    n_strips = 2 * FF_CHUNK // LANES

    def up(c, ubuf_ref):
        u = jnp.dot(h_ref[...], gate_value(wup_ref, slice(None), c), preferred_element_type=F32)
        for k in range(n_strips):
            strip = u[:, k * LANES:(k + 1) * LANES]
            ubuf_ref[k, _spaced(0, FFN_HALO), :] = carry_ref[c, k]
            ubuf_ref[k, _spaced(FFN_HALO, tm), :] = strip
            carry_ref[c, k] = strip[tm - FFN_HALO:tm, :]

    def down(c, ubuf_ref):
        bias = gate_value(dwb_ref, slice(None), c)
        taps = [gate_value(dw_ref, slice(j, j + 1), c) for j in range(FFN_KERNEL)]
        strips = []
        for k in range(n_strips):
            lanes = slice(k * LANES, (k + 1) * LANES)
            y = bias[:, lanes]
            for j in range(FFN_KERNEL):
                y = y + ubuf_ref[k, _spaced(FFN_HALO - (FFN_KERNEL - 1 - j), tm), :] * taps[j][:, lanes]
            strips.append(y)
        y = jnp.concatenate(strips, axis=1)
        gate = y[:, :FF_CHUNK]
        act = (gate * jax.nn.sigmoid(gate) * y[:, FF_CHUNK:]).astype(BF16)
        rows = pl.ds(pl.multiple_of(c * FF_CHUNK, FF_CHUNK), FF_CHUNK)
        o_ref[...] += jnp.dot(act, wdown_ref[rows, :], preferred_element_type=F32)

    up(0, ubuf_a)

    def pair(it, carry):
        c = 2 * it
        up(c + 1, ubuf_b)
        down(c, ubuf_a)
        up(c + 2, ubuf_a)
        down(c + 1, ubuf_b)
        return carry

    assert n_chunks % 2 == 1
    lax.fori_loop(0, n_chunks // 2, pair, 0)
    down(n_chunks - 1, ubuf_a)


def _ffn(x, layer, gain, w_up, dw, dw_b, w_down, attn=None, attn_layer=None, w_attn=None):
    s = x.shape[0]
    tm = FFN_ROW_TILE
    with_attn = attn is not None
    row_spec = pl.BlockSpec((tm, D_MODEL), lambda i: (i, 0))
    args, specs = [x], [row_spec]
    if with_attn:
        args += [attn, w_attn]
        specs += [pl.BlockSpec((tm, GROUP_WIDTH), lambda i: (i, 0)),
                  _layer_spec(attn_layer, (GROUP_WIDTH, D_MODEL))]
    args += [gain, w_up, dw, dw_b, w_down]
    specs += [_layer_spec(layer, (1, D_MODEL)), _layer_spec(layer, (D_MODEL, 2 * D_FF)),
              _layer_spec(layer, (FFN_KERNEL, 2 * D_FF)), _layer_spec(layer, (1, 2 * D_FF)),
              _layer_spec(layer, (D_FF, D_MODEL))]
    return pl.pallas_call(
        functools.partial(_ffn_kernel, with_attn=with_attn),
        grid=(s // tm,),
        in_specs=specs,
        out_specs=row_spec,
        out_shape=jax.ShapeDtypeStruct((s, D_MODEL), F32),
        scratch_shapes=[
            pltpu.VMEM((2 * FF_CHUNK // LANES, 2 * (FFN_HALO + tm), LANES), F32),
            pltpu.VMEM((2 * FF_CHUNK // LANES, 2 * (FFN_HALO + tm), LANES), F32),
            pltpu.VMEM((D_FF // FF_CHUNK, 2 * FF_CHUNK // LANES, FFN_HALO, LANES), F32),
            pltpu.VMEM((tm, D_MODEL), BF16),
        ],
        compiler_params=pltpu.CompilerParams(
            dimension_semantics=("arbitrary",), vmem_limit_bytes=VMEM_LIMIT),
        name="ffn_attn" if with_attn else "ffn",
    )(*args)


def _qkv_kernel(x_ref, g_ref, w_ref, gain_ref, o_ref, hp_ref, hs_ref, quad_ref):
    j = pl.program_id(1)
    tiles_per_group = 3 * GROUP_WIDTH // QKV_COL_TILE
    tiles_per_kind = GROUP_WIDTH // QKV_COL_TILE

    @pl.when(j == 0)
    def _():
        n_strips = D_MODEL // LANES
        for c in range(ATTN_TILE // NORM_ROW_CHUNK):
            rows = slice(c * NORM_ROW_CHUNK, (c + 1) * NORM_ROW_CHUNK)
            h = _rms(x_ref[rows, :], g_ref[...])
            hp_ref[0, rows, :] = h.astype(BF16)
            for k in range(n_strips):
                hs_ref[k, rows, :] = h[:, k * LANES:(k + 1) * LANES]
        for k in range(n_strips):
            lanes = slice(k * LANES, (k + 1) * LANES)
            quad = quad_ref.at[k % 2]
            for b in range(ATTN_TILE // QUAD):
                for r in range(4):
                    dst = slice(b * QUAD + r * BLOCK, b * QUAD + (r + 1) * BLOCK)
                    v = hs_ref[k, pl.ds(b * QUAD + r, BLOCK, stride=4), :]
                    hp_ref[1, dst, lanes] = v.astype(BF16)
                    quad[dst, :] = v
            for r in range(DILATIONS[2]):
                for b in range(ATTN_TILE // QUAD):
                    src = pl.ds(b * QUAD + (r % 4) * BLOCK + r // 4, BLOCK // 4, stride=4)
                    dst = slice(r * BLOCK + b * (BLOCK // 4), r * BLOCK + (b + 1) * (BLOCK // 4))
                    hp_ref[2, dst, lanes] = quad[src, :].astype(BF16)

    grp = j // tiles_per_group
    kind = (j % tiles_per_group) // tiles_per_kind

    is_normed = kind < 2
    w = w_ref[...].astype(BF16)
    for c in range(ATTN_TILE // QKV_ROW_CHUNK):
        rows = slice(c * QKV_ROW_CHUNK, (c + 1) * QKV_ROW_CHUNK)
        acc = jnp.dot(hp_ref[grp, rows, :], w, preferred_element_type=F32)
        for hh in range(QKV_COL_TILE // HEAD_DIM):
            cols = slice(hh * HEAD_DIM, (hh + 1) * HEAD_DIM)
            a = acc[:, cols]
            inv = lax.rsqrt(jnp.mean(a * a, axis=-1, keepdims=True) + EPS)
            o_ref[rows, cols] = (a * jnp.where(is_normed, inv, 1.0) * gain_ref[:, cols]).astype(BF16)


def _qkv(x, layer, attn_layer, gain, w_qkv, q_norm, k_norm):
    s = x.shape[0]
    width = N_GROUPS * 3 * GROUP_WIDTH
    qn = q_norm[attn_layer].reshape(N_GROUPS, 1, GROUP_WIDTH) * (HEAD_DIM ** -0.5)
    kn = k_norm[attn_layer].reshape(N_GROUPS, 1, GROUP_WIDTH)
    head_gain = jnp.concatenate([qn, kn, jnp.ones_like(kn)], axis=1).reshape(1, width)
    return pl.pallas_call(
        _qkv_kernel,
        grid=(s // ATTN_TILE, width // QKV_COL_TILE),
        in_specs=[
            pl.BlockSpec((ATTN_TILE, D_MODEL), lambda i, j: (i, 0)),
            _layer_spec(layer, (1, D_MODEL)),
            pl.BlockSpec((None, D_MODEL, QKV_COL_TILE), lambda i, j: (attn_layer, 0, j)),
            pl.BlockSpec((1, QKV_COL_TILE), lambda i, j: (0, j)),
        ],
        out_specs=pl.BlockSpec((ATTN_TILE, QKV_COL_TILE), lambda i, j: (i, j)),
        out_shape=jax.ShapeDtypeStruct((s, width), BF16),
        scratch_shapes=[
            pltpu.VMEM((N_GROUPS, ATTN_TILE, D_MODEL), BF16),
            pltpu.VMEM((D_MODEL // LANES, ATTN_TILE, LANES), F32),
            pltpu.VMEM((2, ATTN_TILE, LANES), F32),
        ],
        compiler_params=pltpu.CompilerParams(
            dimension_semantics=("arbitrary", "arbitrary"), vmem_limit_bytes=VMEM_LIMIT),
        name="qkv",
    )(x, gain, w_qkv, head_gain)


def _attn_kernel(slopes_ref, *refs):
    q_refs, kc_refs, kp_refs, vc_refs, vp_refs = (refs[3 * n:3 * n + 3] for n in range(5))
    o_ref, acc_ref, den_ref, max_ref, bias_ref = refs[15:]
    tile = pl.program_id(0)
    head = pl.program_id(1)
    n_blocks = ATTN_TILE // BLOCK

    qi = lax.broadcasted_iota(jnp.int32, (BLOCK, 2 * BLOCK), 0)
    ki = lax.broadcasted_iota(jnp.int32, (BLOCK, 2 * BLOCK), 1)
    delta = qi + BLOCK - ki
    in_window = (delta >= 0) & (delta <= N_BACK)
    has_prev = in_window & ((ki >= BLOCK) | (tile > 0))
    for g, d in enumerate(DILATIONS):
        bias = -slopes_ref[g, head] * (delta * d).astype(F32)
        bias_ref[g, 0] = jnp.where(in_window, bias, 1.0)
        bias_ref[g, 1] = jnp.where(has_prev, bias, 1.0)

    ones = jnp.ones((2 * BLOCK, HEAD_DIM), BF16)

    def store(ref, g, n, val):
        if DILATIONS[g] == 16:
            for b in range(ATTN_TILE // QUAD):
                dst = pl.ds(QUAD * b + BLOCK * (n % 4) + n // 4, BLOCK // 4, stride=4)
                ref[g, dst, :] = val[b * (BLOCK // 4):(b + 1) * (BLOCK // 4), :]
        else:
            ref[g, n * BLOCK:(n + 1) * BLOCK, :] = val

    def scores(g, n):
        d = DILATIONS[g]
        rows = slice(n * BLOCK, (n + 1) * BLOCK)
        if n < d:
            prev_k, prev_v = kp_refs[g][rows, :], vp_refs[g][rows, :]
        else:
            back = slice((n - d) * BLOCK, (n - d + 1) * BLOCK)
            prev_k, prev_v = kc_refs[g][back, :], vc_refs[g][back, :]
        k2 = jnp.concatenate([prev_k, kc_refs[g][rows, :]], axis=0)
        sc = lax.dot_general(q_refs[g][rows, :], k2, (((1,), (1,)), ((), ())), preferred_element_type=F32)
        bias = bias_ref[g, 1 if n < d else 0]
        sc = jnp.where(bias <= 0.0, sc + bias, NEG)
        m = jnp.max(sc, axis=-1, keepdims=True)
        store(max_ref, g, n, jnp.broadcast_to(m, (BLOCK, HEAD_DIM)))
        return jnp.exp(sc - m).astype(BF16), prev_v

    def values(g, n, p, prev_v):
        v2 = jnp.concatenate([prev_v, vc_refs[g][n * BLOCK:(n + 1) * BLOCK, :]], axis=0)
        od = jnp.dot(p, jnp.concatenate([v2, ones], axis=1), preferred_element_type=F32)
        store(acc_ref, g, n, od[:, :HEAD_DIM])
        store(den_ref, g, n, od[:, HEAD_DIM:])

    blocks = [(g, n) for g in range(N_GROUPS) for n in range(n_blocks)]
    items = [blocks[i:i + ATTN_UNROLL] for i in range(0, len(blocks), ATTN_UNROLL)]
    pending = [scores(g, n) for g, n in items[0]]
    for i, item in enumerate(items):
        following = [scores(g, n) for g, n in items[i + 1]] if i + 1 < len(items) else []
        for (g, n), (p, prev_v) in zip(item, pending):
            values(g, n, p, prev_v)
        pending = following

    def merge(c, carry):
        seg = lax.shift_right_logical(c, 2) * QUAD + (c & 3)
        in_position_order = pl.ds(seg, BLOCK, stride=4)
        rows = pl.ds(pl.multiple_of(c * BLOCK, BLOCK), BLOCK)
        pick = lambda ref, g: ref[g, in_position_order if DILATIONS[g] == 1 else rows, :]
        tops = [pick(max_ref, g) for g in range(N_GROUPS)]
        top = functools.reduce(jnp.maximum, tops)
        e = [jnp.exp(t - top) for t in tops]
        num = sum(e[g] * pick(acc_ref, g) for g in range(N_GROUPS))
        den = sum(e[g] * pick(den_ref, g) for g in range(N_GROUPS))
        o_ref[in_position_order, :] = num / den
        return carry

    lax.fori_loop(0, n_blocks, merge, 0)


def _attention(qkv):
    s = qkv.shape[0]
    slopes = jnp.asarray(
        2.0 ** (-ALIBI_MAX * (np.arange(N_GROUPS * HEADS, dtype=np.float32) + 1.0) / (N_GROUPS * HEADS)),
        dtype=F32).reshape(N_GROUPS, HEADS)

    def col(g, kind):
        return lambda t, h: (t, (3 * g + kind) * HEADS + h)

    def prev_col(g, kind):
        ratio = ATTN_TILE // (BLOCK * DILATIONS[g])
        return lambda t, h: (jnp.maximum(t * ratio - 1, 0), (3 * g + kind) * HEADS + h)

    cur = lambda kind: [pl.BlockSpec((ATTN_TILE, HEAD_DIM), col(g, kind)) for g in range(N_GROUPS)]
    prev = lambda kind: [pl.BlockSpec((BLOCK * DILATIONS[g], HEAD_DIM), prev_col(g, kind)) for g in range(N_GROUPS)]
    in_specs = ([pl.BlockSpec(memory_space=pltpu.SMEM)] + cur(0) + cur(1) + prev(1) + cur(2) + prev(2))
    return pl.pallas_call(
        _attn_kernel,
        grid=(s // ATTN_TILE, HEADS),
        in_specs=in_specs,
        out_specs=pl.BlockSpec((ATTN_TILE, HEAD_DIM), lambda t, h: (t, h)),
        out_shape=jax.ShapeDtypeStruct((s, GROUP_WIDTH), F32),
        scratch_shapes=[
            pltpu.VMEM((N_GROUPS, ATTN_TILE, HEAD_DIM), F32),
            pltpu.VMEM((N_GROUPS, ATTN_TILE, HEAD_DIM), F32),
            pltpu.VMEM((N_GROUPS, ATTN_TILE, HEAD_DIM), F32),
            pltpu.VMEM((N_GROUPS, 2, BLOCK, 2 * BLOCK), F32),
        ],
        compiler_params=pltpu.CompilerParams(
            dimension_semantics=("arbitrary", "arbitrary"), vmem_limit_bytes=VMEM_LIMIT),
        name="dilated_attention",
    )(slopes, *([qkv] * 15))


def kernel(x, norm_mix, norm_ffn, cm_w_in, cm_b_in, cm_dw, cm_dw_b, cm_ln_g, cm_ln_b, cm_w_out, cm_b_out,
           at_w_qkv, at_q_norm, at_k_norm, at_w_out, ff_w_up, ff_dw, ff_dw_b, ff_w_down):
    batch, seq, _ = x.shape
    assert seq % ATTN_TILE == 0 and x.shape[-1] == D_MODEL
    vec = lambda p: p.reshape(p.shape[0], 1, -1)
    bf16 = lambda p: p.astype(BF16)
    norm_mix, norm_ffn = vec(norm_mix), vec(norm_ffn)
    conformer_params = (bf16(cm_w_in), vec(cm_b_in), cm_dw, vec(cm_dw_b), vec(cm_ln_g), vec(cm_ln_b),
                        bf16(cm_w_out), vec(cm_b_out))
    ffn_params = (bf16(ff_w_up), ff_dw, vec(ff_dw_b), bf16(ff_w_down))
    w_attn_out = bf16(at_w_out)
    outs = []
    for b in range(batch):
        h = x.reshape(seq, D_MODEL) if batch == 1 else x[b]
        h = _conformer(h, 0, 0, norm_mix, *conformer_params)
        h = _ffn(h, 0, norm_ffn, *ffn_params)
        qkv = _qkv(h, 1, 0, norm_mix, at_w_qkv, at_q_norm, at_k_norm)
        attn = _attention(qkv)
        h = _ffn(h, 1, norm_ffn, *ffn_params, attn=attn, attn_layer=0, w_attn=w_attn_out)
        outs.append(h)
    return outs[0].reshape(1, seq, D_MODEL) if batch == 1 else jnp.stack(outs, axis=0)
```

```python
import functools

import numpy as np
import jax
import jax.numpy as jnp
from jax import lax
from jax.experimental import pallas as pl
from jax.experimental.pallas import tpu as pltpu

F32 = jnp.float32
BF16 = jnp.bfloat16

D_MODEL = 1024
HEAD_DIM = 128
HEADS = 8
GROUP_WIDTH = HEADS * HEAD_DIM
DILATIONS = (1, 4, 16)
N_GROUPS = len(DILATIONS)
N_BACK = 128
BLOCK = 128
ATTN_TILE = BLOCK * DILATIONS[-1]
QUAD = 4 * BLOCK
ALIBI_MAX = 8.0
CONV_KERNEL = 31
CONV_HALO = 32
FFN_KERNEL = 3
FFN_HALO = 8
D_FF = 2816
FF_CHUNK = 256
EPS = 1e-6
NEG = -1e30
LANES = 128
VMEM_LIMIT = 60 * 1024 * 1024

CONF_ROW_TILE = 512
CONF_SUB = 256
CONV_ROWS = 64
FFN_ROW_TILE = 512
FFN_DOWN_GROUP = 4
QKV_COL_TILE = 1024
QKV_ROW_CHUNK = 512
NORM_ROW_CHUNK = 256
ATTN_UNROLL = 4


def _rms(x, gain):
    return x * lax.rsqrt(jnp.mean(x * x, axis=-1, keepdims=True) + EPS) * gain


def _spaced(start, size):
    return pl.ds(2 * start, size, stride=2)


def _layer_spec(layer, shape):
    index = (layer,) + (0,) * len(shape)
    return pl.BlockSpec((None, *shape), lambda *_: index, pipeline_mode=pl.Buffered(1))


def _ffn_stages(h_ref, wup_ref, dw_ref, dwb_ref, wdown_ref, o_ref, ubuf_a, ubuf_b, carry_ref, act_ref, tm):
    n_chunks = D_FF // FF_CHUNK
    n_strips = 2 * FF_CHUNK // LANES
    buffers = (ubuf_a, ubuf_b)

    def gate_value(ref, rows, c):
        gate = pl.ds(pl.multiple_of(c * FF_CHUNK, FF_CHUNK), FF_CHUNK)
        value = pl.ds(pl.multiple_of(D_FF + c * FF_CHUNK, FF_CHUNK), FF_CHUNK)
        return jnp.concatenate([ref[rows, gate], ref[rows, value]], axis=1)

    def up(c, ubuf_ref):
        u = jnp.dot(h_ref[...], gate_value(wup_ref, slice(None), c), preferred_element_type=F32)
        for k in range(n_strips):
            strip = u[:, k * LANES:(k + 1) * LANES]
            ubuf_ref[k, _spaced(0, FFN_HALO), :] = carry_ref[c, k]
            ubuf_ref[k, _spaced(FFN_HALO, tm), :] = strip
            carry_ref[c, k] = strip[tm - FFN_HALO:tm, :]

    def gate(c, ubuf_ref, slot):
        bias = gate_value(dwb_ref, slice(None), c)
        taps = [gate_value(dw_ref, slice(j, j + 1), c) for j in range(FFN_KERNEL)]
        strips = []
        for k in range(n_strips):
            lanes = slice(k * LANES, (k + 1) * LANES)
            y = bias[:, lanes]
            for j in range(FFN_KERNEL):
                y = y + ubuf_ref[k, _spaced(FFN_HALO - (FFN_KERNEL - 1 - j), tm), :] * taps[j][:, lanes]
            strips.append(y)
        y = jnp.concatenate(strips, axis=1)
        g = y[:, :FF_CHUNK]
        act_ref[:, slot * FF_CHUNK:(slot + 1) * FF_CHUNK] = (g * jax.nn.sigmoid(g) * y[:, FF_CHUNK:]).astype(BF16)

    def down(first, count):
        rows = pl.ds(pl.multiple_of(first * FF_CHUNK, FF_CHUNK), count * FF_CHUNK)
        o_ref[...] += jnp.dot(act_ref[:, :count * FF_CHUNK], wdown_ref[rows, :], preferred_element_type=F32)

    assert FFN_DOWN_GROUP % 2 == 0
    full_groups, n_tail = divmod(n_chunks, FFN_DOWN_GROUP)
    assert n_tail > 0

    def head():
        up(0, ubuf_a)

    def group(it):
        first = FFN_DOWN_GROUP * it
        for slot in range(FFN_DOWN_GROUP):
            up(first + slot + 1, buffers[(slot + 1) % 2])
            gate(first + slot, buffers[slot % 2], slot)
        down(first, FFN_DOWN_GROUP)

    def tail(between=()):
        first = full_groups * FFN_DOWN_GROUP
        for slot in range(n_tail):
            if slot + 1 < n_tail:
                up(first + slot + 1, buffers[(slot + 1) % 2])
            gate(first + slot, buffers[slot % 2], slot)
            if slot < len(between):
                between[slot]()
        down(first, n_tail)

    return head, group, tail, full_groups


def _ffn_scratch(tm):
    strips = 2 * FF_CHUNK // LANES
    return [
        pltpu.VMEM((strips, 2 * (FFN_HALO + tm), LANES), F32),
        pltpu.VMEM((strips, 2 * (FFN_HALO + tm), LANES), F32),
        pltpu.VMEM((D_FF // FF_CHUNK, strips, FFN_HALO, LANES), F32),
        pltpu.VMEM((tm, D_MODEL), BF16),
        pltpu.VMEM((tm, FFN_DOWN_GROUP * FF_CHUNK), BF16),
    ]


def _conformer_ffn_kernel(x_ref, xnext_ref, gmix_ref, win_ref, bin_ref, cdw_ref, cdwb_ref, lng_ref, lnb_ref,
                          wout_ref, bout_ref, gffn_ref, wup_ref, fdw_ref, fdwb_ref, wdown_ref, o_ref,
                          x1_ref, buf_ref, y_ref, ubuf_a, ubuf_b, carry_ref, h_ref, act_ref):
    tm = x_ref.shape[0]
    n_sub = tm // CONF_SUB
    n_strips = D_MODEL // LANES
    n_units = n_sub * n_strips

    def glu(src_ref):
        for s in range(n_sub):
            rows = slice(s * CONF_SUB, (s + 1) * CONF_SUB)
            h = _rms(src_ref[rows, :], gmix_ref[...]).astype(BF16)
            u = jnp.dot(h, win_ref[...], preferred_element_type=F32) + bin_ref[...]
            g = u[:, :D_MODEL] * jax.nn.sigmoid(u[:, D_MODEL:])
            for c in range(n_strips):
                buf_ref[c, _spaced(CONV_HALO + s * CONF_SUB, CONF_SUB), :] = g[:, c * LANES:(c + 1) * LANES]

    @pl.when(pl.program_id(0) == 0)
    def _():
        for c in range(n_strips):
            buf_ref[c, _spaced(0, CONV_HALO), :] = jnp.zeros((CONV_HALO, LANES), F32)
        carry_ref[...] = jnp.zeros(carry_ref.shape, F32)
        x1_ref[...] = jnp.zeros(x1_ref.shape, F32)
        glu(x_ref)

    head, group, tail, full_groups = _ffn_stages(
        h_ref, wup_ref, fdw_ref, fdwb_ref, wdown_ref, o_ref, ubuf_a, ubuf_b, carry_ref, act_ref, tm)

    for r in range(tm // NORM_ROW_CHUNK):
        rows = slice(r * NORM_ROW_CHUNK, (r + 1) * NORM_ROW_CHUNK)
        x1 = x1_ref[rows, :]
        o_ref[rows, :] = x1
        h_ref[rows, :] = _rms(x1, gffn_ref[...]).astype(BF16)

    head()

    def conv_unit(unit):
        sub, strip = lax.shift_right_logical(unit, 3), unit & (n_strips - 1)
        lanes = pl.ds(pl.multiple_of(strip * LANES, LANES), LANES)
        for q in range(CONF_SUB // CONV_ROWS):
            first = pl.multiple_of(sub * CONF_SUB + q * CONV_ROWS, CONV_ROWS)
            y = cdwb_ref[:, lanes]
            for m in range(CONV_KERNEL):
                w = cdw_ref[CONV_KERNEL - 1 - m:CONV_KERNEL - m, lanes]
                y = y + buf_ref[strip, _spaced(CONV_HALO + first - m, CONV_ROWS), :] * w
            y_ref[pl.ds(first, CONV_ROWS), lanes] = y

    assert n_strips == 8 and n_units % full_groups == 0
    units_per_group = n_units // full_groups

    def body(it, carry):
        for j in range(units_per_group):
            conv_unit(it * units_per_group + j)
        group(it)
        return carry

    lax.fori_loop(0, full_groups, body, 0)

    def project(s):
        def run():
            rows = slice(s * CONF_SUB, (s + 1) * CONF_SUB)
            y = y_ref[rows, :]
            mu = jnp.mean(y, axis=-1, keepdims=True)
            yc = y - mu
            var = jnp.mean(yc * yc, axis=-1, keepdims=True)
            z = yc * lax.rsqrt(var + EPS) * lng_ref[...] + lnb_ref[...]
            z = (z * jax.nn.sigmoid(z)).astype(BF16)
            x1_ref[rows, :] = x_ref[rows, :] + jnp.dot(z, wout_ref[...], preferred_element_type=F32) + bout_ref[...]
        return run

    def next_glu():
        for c in range(n_strips):
            buf_ref[c, _spaced(0, CONV_HALO), :] = buf_ref[c, _spaced(tm, CONV_HALO), :]
        glu(xnext_ref)

    assert n_sub == 2
    tail(between=(project(0), project(1), next_glu))


def _conformer_ffn(x, gmix, w_in, b_in, cdw, cdw_b, ln_g, ln_b, w_out, b_out, gffn, w_up, fdw, fdw_b, w_down):
    s = x.shape[0]
    tm = CONF_ROW_TILE
    n_tiles = s // tm
    return pl.pallas_call(
        _conformer_ffn_kernel,
        grid=(n_tiles + 1,),
        in_specs=[
            pl.BlockSpec((tm, D_MODEL), lambda i: (jnp.minimum(i, n_tiles - 1), 0)),
            pl.BlockSpec((tm, D_MODEL), lambda i: (jnp.minimum(i + 1, n_tiles - 1), 0)),
            _layer_spec(0, (1, D_MODEL)),
            _layer_spec(0, (D_MODEL, 2 * D_MODEL)),
            _layer_spec(0, (1, 2 * D_MODEL)),
            _layer_spec(0, (CONV_KERNEL, D_MODEL)),
            _layer_spec(0, (1, D_MODEL)),
            _layer_spec(0, (1, D_MODEL)),
            _layer_spec(0, (1, D_MODEL)),
            _layer_spec(0, (D_MODEL, D_MODEL)),
            _layer_spec(0, (1, D_MODEL)),
            _layer_spec(0, (1, D_MODEL)),
            _layer_spec(0, (D_MODEL, 2 * D_FF)),
            _layer_spec(0, (FFN_KERNEL, 2 * D_FF)),
            _layer_spec(0, (1, 2 * D_FF)),
            _layer_spec(0, (D_FF, D_MODEL)),
        ],
        out_specs=pl.BlockSpec((tm, D_MODEL), lambda i: (jnp.maximum(i - 1, 0), 0)),
        out_shape=jax.ShapeDtypeStruct((s, D_MODEL), F32),
        scratch_shapes=[
            pltpu.VMEM((tm, D_MODEL), F32),
            pltpu.VMEM((D_MODEL // LANES, 2 * (CONV_HALO + tm), LANES), F32),
            pltpu.VMEM((tm, D_MODEL), F32),
        ] + _ffn_scratch(tm),
        compiler_params=pltpu.CompilerParams(
            dimension_semantics=("arbitrary",), vmem_limit_bytes=VMEM_LIMIT),
        name="conformer_ffn",
    )(x, x, gmix, w_in, b_in, cdw, cdw_b, ln_g, ln_b, w_out, b_out, gffn, w_up, fdw, fdw_b, w_down)


def _attn_ffn_kernel(x_ref, a_ref, wattn_ref, g_ref, wup_ref, dw_ref, dwb_ref, wdown_ref, o_ref,
                     ubuf_a, ubuf_b, carry_ref, h_ref, act_ref):
    tm = x_ref.shape[0]

    @pl.when(pl.program_id(0) == 0)
    def _():
        carry_ref[...] = jnp.zeros(carry_ref.shape, F32)

    x = x_ref[...] + jnp.dot(a_ref[...].astype(BF16), wattn_ref[...], preferred_element_type=F32)
    o_ref[...] = x
    h_ref[...] = _rms(x, g_ref[...]).astype(BF16)

    head, group, tail, full_groups = _ffn_stages(
        h_ref, wup_ref, dw_ref, dwb_ref, wdown_ref, o_ref, ubuf_a, ubuf_b, carry_ref, act_ref, tm)
    head()

    def body(it, carry):
        group(it)
        return carry

    lax.fori_loop(0, full_groups, body, 0)
    tail()


def _attn_ffn(x, attn, w_attn, gain, w_up, dw, dw_b, w_down):
    s = x.shape[0]
    tm = FFN_ROW_TILE
    row_spec = pl.BlockSpec((tm, D_MODEL), lambda i: (i, 0))
    return pl.pallas_call(
        _attn_ffn_kernel,
        grid=(s // tm,),
        in_specs=[
            row_spec,
            pl.BlockSpec((tm, GROUP_WIDTH), lambda i: (i, 0)),
            _layer_spec(0, (GROUP_WIDTH, D_MODEL)),
            _layer_spec(1, (1, D_MODEL)),
            _layer_spec(1, (D_MODEL, 2 * D_FF)),
            _layer_spec(1, (FFN_KERNEL, 2 * D_FF)),
            _layer_spec(1, (1, 2 * D_FF)),
            _layer_spec(1, (D_FF, D_MODEL)),
        ],
        out_specs=row_spec,
        out_shape=jax.ShapeDtypeStruct((s, D_MODEL), F32),
        scratch_shapes=_ffn_scratch(tm),
        compiler_params=pltpu.CompilerParams(
            dimension_semantics=("arbitrary",), vmem_limit_bytes=VMEM_LIMIT),
        name="ffn_attn",
    )(x, attn, w_attn, gain, w_up, dw, dw_b, w_down)


def _qkv_kernel(x_ref, g_ref, w_ref, gain_ref, o_ref, hp_ref, hs_ref, quad_ref):
    j = pl.program_id(1)
    tiles_per_group = 3 * GROUP_WIDTH // QKV_COL_TILE
    tiles_per_kind = GROUP_WIDTH // QKV_COL_TILE

    @pl.when(j == 0)
    def _():
        n_strips = D_MODEL // LANES
        for c in range(ATTN_TILE // NORM_ROW_CHUNK):
            rows = slice(c * NORM_ROW_CHUNK, (c + 1) * NORM_ROW_CHUNK)
            h = _rms(x_ref[rows, :], g_ref[...])
            hp_ref[0, rows, :] = h.astype(BF16)
            for k in range(n_strips):
                hs_ref[k, rows, :] = h[:, k * LANES:(k + 1) * LANES]
        for k in range(n_strips):
            lanes = slice(k * LANES, (k + 1) * LANES)
            quad = quad_ref.at[k % 2]
            for b in range(ATTN_TILE // QUAD):
                for r in range(4):
                    dst = slice(b * QUAD + r * BLOCK, b * QUAD + (r + 1) * BLOCK)
                    v = hs_ref[k, pl.ds(b * QUAD + r, BLOCK, stride=4), :]
                    hp_ref[1, dst, lanes] = v.astype(BF16)
                    quad[dst, :] = v
            for r in range(DILATIONS[2]):
                for b in range(ATTN_TILE // QUAD):
                    src = pl.ds(b * QUAD + (r % 4) * BLOCK + r // 4, BLOCK // 4, stride=4)
                    dst = slice(r * BLOCK + b * (BLOCK // 4), r * BLOCK + (b + 1) * (BLOCK // 4))
                    hp_ref[2, dst, lanes] = quad[src, :].astype(BF16)

    grp = j // tiles_per_group
    kind = (j % tiles_per_group) // tiles_per_kind

    is_normed = kind < 2
    w = w_ref[...].astype(BF16)
    for c in range(ATTN_TILE // QKV_ROW_CHUNK):
        rows = slice(c * QKV_ROW_CHUNK, (c + 1) * QKV_ROW_CHUNK)
        acc = jnp.dot(hp_ref[grp, rows, :], w, preferred_element_type=F32)
        for hh in range(QKV_COL_TILE // HEAD_DIM):
            cols = slice(hh * HEAD_DIM, (hh + 1) * HEAD_DIM)
            a = acc[:, cols]
            inv = lax.rsqrt(jnp.mean(a * a, axis=-1, keepdims=True) + EPS)
            o_ref[rows, cols] = (a * jnp.where(is_normed, inv, 1.0) * gain_ref[:, cols]).astype(BF16)


def _qkv(x, layer, attn_layer, gain, w_qkv, q_norm, k_norm):
    s = x.shape[0]
    width = N_GROUPS * 3 * GROUP_WIDTH
    qn = q_norm[attn_layer].reshape(N_GROUPS, 1, GROUP_WIDTH) * (HEAD_DIM ** -0.5)
    kn = k_norm[attn_layer].reshape(N_GROUPS, 1, GROUP_WIDTH)
    head_gain = jnp.concatenate([qn, kn, jnp.ones_like(kn)], axis=1).reshape(1, width)
    return pl.pallas_call(
        _qkv_kernel,
        grid=(s // ATTN_TILE, width // QKV_COL_TILE),
        in_specs=[
            pl.BlockSpec((ATTN_TILE, D_MODEL), lambda i, j: (i, 0)),
            _layer_spec(layer, (1, D_MODEL)),
            pl.BlockSpec((None, D_MODEL, QKV_COL_TILE), lambda i, j: (attn_layer, 0, j)),
            pl.BlockSpec((1, QKV_COL_TILE), lambda i, j: (0, j)),
        ],
        out_specs=pl.BlockSpec((ATTN_TILE, QKV_COL_TILE), lambda i, j: (i, j)),
        out_shape=jax.ShapeDtypeStruct((s, width), BF16),
        scratch_shapes=[
            pltpu.VMEM((N_GROUPS, ATTN_TILE, D_MODEL), BF16),
            pltpu.VMEM((D_MODEL // LANES, ATTN_TILE, LANES), F32),
            pltpu.VMEM((2, ATTN_TILE, LANES), F32),
        ],
        compiler_params=pltpu.CompilerParams(
            dimension_semantics=("arbitrary", "arbitrary"), vmem_limit_bytes=VMEM_LIMIT),
        name="qkv",
    )(x, gain, w_qkv, head_gain)


def _attn_kernel(slopes_ref, *refs):
    q_refs, kc_refs, kp_refs, vc_refs, vp_refs = (refs[3 * n:3 * n + 3] for n in range(5))
    o_ref, acc_ref, den_ref, max_ref, bias_ref = refs[15:]
    tile = pl.program_id(0)
    head = pl.program_id(1)
    n_blocks = ATTN_TILE // BLOCK

    qi = lax.broadcasted_iota(jnp.int32, (BLOCK, 2 * BLOCK), 0)
    ki = lax.broadcasted_iota(jnp.int32, (BLOCK, 2 * BLOCK), 1)
    delta = qi + BLOCK - ki
    in_window = (delta >= 0) & (delta <= N_BACK)
    has_prev = in_window & ((ki >= BLOCK) | (tile > 0))
    for g, d in enumerate(DILATIONS):
        bias = -slopes_ref[g, head] * (delta * d).astype(F32)
        bias_ref[g, 0] = jnp.where(in_window, bias, 1.0)
        bias_ref[g, 1] = jnp.where(has_prev, bias, 1.0)

    ones = jnp.ones((2 * BLOCK, HEAD_DIM), BF16)

    def store(ref, g, n, val):
        if DILATIONS[g] == 16:
            for b in range(ATTN_TILE // QUAD):
                dst = pl.ds(QUAD * b + BLOCK * (n % 4) + n // 4, BLOCK // 4, stride=4)
                ref[g, dst, :] = val[b * (BLOCK // 4):(b + 1) * (BLOCK // 4), :]
        else:
            ref[g, n * BLOCK:(n + 1) * BLOCK, :] = val

    def scores(g, n):
        d = DILATIONS[g]
        rows = slice(n * BLOCK, (n + 1) * BLOCK)
        if n < d:
            prev_k, prev_v = kp_refs[g][rows, :], vp_refs[g][rows, :]
        else:
            back = slice((n - d) * BLOCK, (n - d + 1) * BLOCK)
            prev_k, prev_v = kc_refs[g][back, :], vc_refs[g][back, :]
        k2 = jnp.concatenate([prev_k, kc_refs[g][rows, :]], axis=0)
        sc = lax.dot_general(q_refs[g][rows, :], k2, (((1,), (1,)), ((), ())), preferred_element_type=F32)
        bias = bias_ref[g, 1 if n < d else 0]
        sc = jnp.where(bias <= 0.0, sc + bias, NEG)
        m = jnp.max(sc, axis=-1, keepdims=True)
        store(max_ref, g, n, jnp.broadcast_to(m, (BLOCK, HEAD_DIM)))
        return jnp.exp(sc - m).astype(BF16), prev_v

    def values(g, n, p, prev_v):
        v2 = jnp.concatenate([prev_v, vc_refs[g][n * BLOCK:(n + 1) * BLOCK, :]], axis=0)
        od = jnp.dot(p, jnp.concatenate([v2, ones], axis=1), preferred_element_type=F32)
        store(acc_ref, g, n, od[:, :HEAD_DIM])
        store(den_ref, g, n, od[:, HEAD_DIM:])

    def merge(c):
        in_position_order = pl.ds((c // 4) * QUAD + c % 4, BLOCK, stride=4)
        rows = slice(c * BLOCK, (c + 1) * BLOCK)
        pick = lambda ref, g: ref[g, in_position_order if DILATIONS[g] == 1 else rows, :]
        tops = [pick(max_ref, g) for g in range(N_GROUPS)]
        top = functools.reduce(jnp.maximum, tops)
        e = [jnp.exp(t - top) for t in tops]
        num = sum(e[g] * pick(acc_ref, g) for g in range(N_GROUPS))
        den = sum(e[g] * pick(den_ref, g) for g in range(N_GROUPS))
        o_ref[in_position_order, :] = num / den

    assert ATTN_UNROLL == QUAD // BLOCK and DILATIONS[0] == 1
    blocks = [(g, n) for g in reversed(range(N_GROUPS)) for n in range(n_blocks)]
    items = [blocks[i:i + ATTN_UNROLL] for i in range(0, len(blocks), ATTN_UNROLL)]
    pending = [scores(g, n) for g, n in items[0]]
    for i, item in enumerate(items):
        following = [scores(g, n) for g, n in items[i + 1]] if i + 1 < len(items) else []
        for (g, n), (p, prev_v) in zip(item, pending):
            values(g, n, p, prev_v)
        pending = following
        if item[0][0] == 0:
            for g, n in item:
                merge(n)


def _attention(qkv):
    s = qkv.shape[0]
    slopes = jnp.asarray(
        2.0 ** (-ALIBI_MAX * (np.arange(N_GROUPS * HEADS, dtype=np.float32) + 1.0) / (N_GROUPS * HEADS)),
        dtype=F32).reshape(N_GROUPS, HEADS)

    def col(g, kind):
        return lambda t, h: (t, (3 * g + kind) * HEADS + h)

    def prev_col(g, kind):
        ratio = ATTN_TILE // (BLOCK * DILATIONS[g])
        return lambda t, h: (jnp.maximum(t * ratio - 1, 0), (3 * g + kind) * HEADS + h)

    cur = lambda kind: [pl.BlockSpec((ATTN_TILE, HEAD_DIM), col(g, kind)) for g in range(N_GROUPS)]
    prev = lambda kind: [pl.BlockSpec((BLOCK * DILATIONS[g], HEAD_DIM), prev_col(g, kind)) for g in range(N_GROUPS)]
    in_specs = ([pl.BlockSpec(memory_space=pltpu.SMEM)] + cur(0) + cur(1) + prev(1) + cur(2) + prev(2))
    return pl.pallas_call(
        _attn_kernel,
        grid=(s // ATTN_TILE, HEADS),
        in_specs=in_specs,
        out_specs=pl.BlockSpec((ATTN_TILE, HEAD_DIM), lambda t, h: (t, h)),
        out_shape=jax.ShapeDtypeStruct((s, GROUP_WIDTH), F32),
        scratch_shapes=[
            pltpu.VMEM((N_GROUPS, ATTN_TILE, HEAD_DIM), F32),
            pltpu.VMEM((N_GROUPS, ATTN_TILE, HEAD_DIM), F32),
            pltpu.VMEM((N_GROUPS, ATTN_TILE, HEAD_DIM), F32),
            pltpu.VMEM((N_GROUPS, 2, BLOCK, 2 * BLOCK), F32),
        ],
        compiler_params=pltpu.CompilerParams(
            dimension_semantics=("arbitrary", "arbitrary"), vmem_limit_bytes=VMEM_LIMIT),
        name="dilated_attention",
    )(slopes, *([qkv] * 15))


def kernel(x, norm_mix, norm_ffn, cm_w_in, cm_b_in, cm_dw, cm_dw_b, cm_ln_g, cm_ln_b, cm_w_out, cm_b_out,
           at_w_qkv, at_q_norm, at_k_norm, at_w_out, ff_w_up, ff_dw, ff_dw_b, ff_w_down):
    batch, seq, _ = x.shape
    assert seq % ATTN_TILE == 0 and x.shape[-1] == D_MODEL
    vec = lambda p: p.reshape(p.shape[0], 1, -1)
    bf16 = lambda p: p.astype(BF16)
    norm_mix, norm_ffn = vec(norm_mix), vec(norm_ffn)
    conformer_params = (bf16(cm_w_in), vec(cm_b_in), cm_dw, vec(cm_dw_b), vec(cm_ln_g), vec(cm_ln_b),
                        bf16(cm_w_out), vec(cm_b_out))
    ffn_params = (bf16(ff_w_up), ff_dw, vec(ff_dw_b), bf16(ff_w_down))
    w_attn_out = bf16(at_w_out)
    outs = []
    for b in range(batch):
        h = x.reshape(seq, D_MODEL) if batch == 1 else x[b]
        h = _conformer_ffn(h, norm_mix, *conformer_params, norm_ffn, *ffn_params)
        qkv = _qkv(h, 1, 0, norm_mix, at_w_qkv, at_q_norm, at_k_norm)
        attn = _attention(qkv)
        h = _attn_ffn(h, attn, w_attn_out, norm_ffn, *ffn_params)
        outs.append(h)
    return outs[0].reshape(1, seq, D_MODEL) if batch == 1 else jnp.stack(outs, axis=0)
```

```python
import functools

import numpy as np
import jax
import jax.numpy as jnp
from jax import lax
from jax.experimental import pallas as pl
from jax.experimental.pallas import tpu as pltpu

F32 = jnp.float32
BF16 = jnp.bfloat16

D_MODEL = 1024
HEAD_DIM = 128
HEADS = 8
GROUP_WIDTH = HEADS * HEAD_DIM
DILATIONS = (1, 4, 16)
N_GROUPS = len(DILATIONS)
N_BACK = 128
BLOCK = 128
ATTN_TILE = BLOCK * DILATIONS[-1]
QUAD = 4 * BLOCK
ALIBI_MAX = 8.0
CONV_KERNEL = 31
CONV_HALO = 32
FFN_KERNEL = 3
FFN_HALO = 8
D_FF = 2816
FF_CHUNK = 256
EPS = 1e-6
NEG = -1e30
LANES = 128
SUBLANES = 8
VMEM_LIMIT = 60 * 1024 * 1024

CONF_ROW_TILE = 1024
CONF_SUB = 256
CONV_ROWS = 64
FFN_ROW_TILE = 1024
QKV_COL_TILE = 1024
QKV_ROW_CHUNK = 512
NORM_ROW_CHUNK = 256
ATTN_UNROLL = 4


def _rms(x, gain):
    return x * lax.rsqrt(jnp.mean(x * x, axis=-1, keepdims=True) + EPS) * gain


def _spaced(start, size):
    return pl.ds(2 * start, size, stride=2)


def _layer_spec(layer, shape):
    index = (layer,) + (0,) * len(shape)
    return pl.BlockSpec((None, *shape), lambda *_: index, pipeline_mode=pl.Buffered(1))


def _conformer_kernel(x_ref, gmix_ref, win_ref, bin_ref, dw_ref, dwb_ref, lng_ref, lnb_ref, wout_ref, bout_ref,
                      o_ref, buf_ref, y_ref):
    tm = x_ref.shape[0]
    n_sub = tm // CONF_SUB

    @pl.when(pl.program_id(0) == 0)
    def _():
        for c in range(D_MODEL // LANES):
            buf_ref[c, _spaced(0, CONV_HALO), :] = jnp.zeros((CONV_HALO, LANES), F32)

    def glu(s):
        first = pl.multiple_of(s * CONF_SUB, CONF_SUB)
        h = _rms(x_ref[pl.ds(first, CONF_SUB), :], gmix_ref[...]).astype(BF16)
        u = jnp.dot(h, win_ref[...], preferred_element_type=F32) + bin_ref[...]
        g = u[:, :D_MODEL] * jax.nn.sigmoid(u[:, D_MODEL:])
        for c in range(D_MODEL // LANES):
            buf_ref[c, _spaced(CONV_HALO + first, CONF_SUB), :] = g[:, c * LANES:(c + 1) * LANES]

    def conv(s):
        reach = SUBLANES * ((CONV_KERNEL - 1) // SUBLANES)
        for c in range(D_MODEL // LANES):
            lanes = slice(c * LANES, (c + 1) * LANES)
            for q in range(CONF_SUB // CONV_ROWS):
                first = pl.multiple_of(s * CONF_SUB + q * CONV_ROWS, CONV_ROWS)
                y = dwb_ref[:, lanes]
                for j in range(SUBLANES):
                    window = buf_ref[c, _spaced(CONV_HALO + first - j - reach, CONV_ROWS + reach), :]
                    for m in range(j, CONV_KERNEL, SUBLANES):
                        w = dw_ref[CONV_KERNEL - 1 - m:CONV_KERNEL - m, lanes]
                        y = y + window[reach - (m - j):reach - (m - j) + CONV_ROWS, :] * w
                y_ref[pl.ds(first, CONV_ROWS), lanes] = y

    def project(s):
        rows = pl.ds(pl.multiple_of(s * CONF_SUB, CONF_SUB), CONF_SUB)
        y = y_ref[rows, :]
        mu = jnp.mean(y, axis=-1, keepdims=True)
        yc = y - mu
        var = jnp.mean(yc * yc, axis=-1, keepdims=True)
        z = yc * lax.rsqrt(var + EPS) * lng_ref[...] + lnb_ref[...]
        z = (z * jax.nn.sigmoid(z)).astype(BF16)
        o_ref[rows, :] = x_ref[rows, :] + jnp.dot(z, wout_ref[...], preferred_element_type=F32) + bout_ref[...]

    glu(0)

    def body(s, carry):
        conv(s)
        glu(s + 1)
        project(s)
        return carry

    lax.fori_loop(0, n_sub - 1, body, 0)
    conv(n_sub - 1)
    project(n_sub - 1)
    for c in range(D_MODEL // LANES):
        buf_ref[c, _spaced(0, CONV_HALO), :] = buf_ref[c, _spaced(tm, CONV_HALO), :]


def _conformer(x, layer, conv_layer, gmix, w_in, b_in, dw, dw_b, ln_g, ln_b, w_out, b_out):
    s = x.shape[0]
    tm = CONF_ROW_TILE
    return pl.pallas_call(
        _conformer_kernel,
        grid=(s // tm,),
        in_specs=[
            pl.BlockSpec((tm, D_MODEL), lambda i: (i, 0)),
            _layer_spec(layer, (1, D_MODEL)),
            _layer_spec(conv_layer, (D_MODEL, 2 * D_MODEL)),
            _layer_spec(conv_layer, (1, 2 * D_MODEL)),
            _layer_spec(conv_layer, (CONV_KERNEL, D_MODEL)),
            _layer_spec(conv_layer, (1, D_MODEL)),
            _layer_spec(conv_layer, (1, D_MODEL)),
            _layer_spec(conv_layer, (1, D_MODEL)),
            _layer_spec(conv_layer, (D_MODEL, D_MODEL)),
            _layer_spec(conv_layer, (1, D_MODEL)),
        ],
        out_specs=pl.BlockSpec((tm, D_MODEL), lambda i: (i, 0)),
        out_shape=jax.ShapeDtypeStruct((s, D_MODEL), F32),
        scratch_shapes=[
            pltpu.VMEM((D_MODEL // LANES, 2 * (CONV_HALO + tm), LANES), F32),
            pltpu.VMEM((tm, D_MODEL), F32),
        ],
        compiler_params=pltpu.CompilerParams(
            dimension_semantics=("arbitrary",), vmem_limit_bytes=VMEM_LIMIT),
        name="conformer",
    )(x, gmix, w_in, b_in, dw, dw_b, ln_g, ln_b, w_out, b_out)


def _ffn_kernel(*refs, with_attn):
    if with_attn:
        x_ref, a_ref, wattn_ref, *refs = refs
    else:
        x_ref, *refs = refs
    g_ref, wup_ref, dw_ref, dwb_ref, wdown_ref, o_ref, ubuf_a, ubuf_b, carry_ref, h_ref = refs
    tm = x_ref.shape[0]
    n_chunks = D_FF // FF_CHUNK

    @pl.when(pl.program_id(0) == 0)
    def _():
        carry_ref[...] = jnp.zeros(carry_ref.shape, F32)

    x = x_ref[...]
    if with_attn:
        x = x + jnp.dot(a_ref[...].astype(BF16), wattn_ref[...], preferred_element_type=F32)
    o_ref[...] = x
    h_ref[...] = _rms(x, g_ref[...]).astype(BF16)

    def gate_value(ref, rows, c):
        gate = pl.ds(pl.multiple_of(c * FF_CHUNK, FF_CHUNK), FF_CHUNK)
        value = pl.ds(pl.multiple_of(D_FF + c * FF_CHUNK, FF_CHUNK), FF_CHUNK)
        return jnp.concatenate([ref[rows, gate], ref[rows, value]], axis=1)

    n_strips = 2 * FF_CHUNK // LANES

    def up(c, ubuf_ref):
        u = jnp.dot(h_ref[...], gate_value(wup_ref, slice(None), c), preferred_element_type=F32)
        for k in range(n_strips):
            strip = u[:, k * LANES:(k + 1) * LANES]
            ubuf_ref[k, _spaced(0, FFN_HALO), :] = carry_ref[c, k]
            ubuf_ref[k, _spaced(FFN_HALO, tm), :] = strip
            carry_ref[c, k] = strip[tm - FFN_HALO:tm, :]

    def down(c, ubuf_ref):
        bias = gate_value(dwb_ref, slice(None), c)
        taps = [gate_value(dw_ref, slice(j, j + 1), c) for j in range(FFN_KERNEL)]
        strips = []
        for k in range(n_strips):
            lanes = slice(k * LANES, (k + 1) * LANES)
            y = bias[:, lanes]
            for j in range(FFN_KERNEL):
                y = y + ubuf_ref[k, _spaced(FFN_HALO - (FFN_KERNEL - 1 - j), tm), :] * taps[j][:, lanes]
            strips.append(y)
        y = jnp.concatenate(strips, axis=1)
        gate = y[:, :FF_CHUNK]
        act = (gate * jax.nn.sigmoid(gate) * y[:, FF_CHUNK:]).astype(BF16)
        rows = pl.ds(pl.multiple_of(c * FF_CHUNK, FF_CHUNK), FF_CHUNK)
        o_ref[...] += jnp.dot(act, wdown_ref[rows, :], preferred_element_type=F32)

    up(0, ubuf_a)

    def pair(it, carry):
        c = 2 * it
        up(c + 1, ubuf_b)
        down(c, ubuf_a)
        up(c + 2, ubuf_a)
        down(c + 1, ubuf_b)
        return carry

    assert n_chunks % 2 == 1
    lax.fori_loop(0, n_chunks // 2, pair, 0)
    down(n_chunks - 1, ubuf_a)


def _ffn(x, layer, gain, w_up, dw, dw_b, w_down, attn=None, attn_layer=None, w_attn=None):
    s = x.shape[0]
    tm = FFN_ROW_TILE
    with_attn = attn is not None
    row_spec = pl.BlockSpec((tm, D_MODEL), lambda i: (i, 0))
    args, specs = [x], [row_spec]
    if with_attn:
        args += [attn, w_attn]
        specs += [pl.BlockSpec((tm, GROUP_WIDTH), lambda i: (i, 0)),
                  _layer_spec(attn_layer, (GROUP_WIDTH, D_MODEL))]
    args += [gain, w_up, dw, dw_b, w_down]
    specs += [_layer_spec(layer, (1, D_MODEL)), _layer_spec(layer, (D_MODEL, 2 * D_FF)),
              _layer_spec(layer, (FFN_KERNEL, 2 * D_FF)), _layer_spec(layer, (1, 2 * D_FF)),
              _layer_spec(layer, (D_FF, D_MODEL))]
    return pl.pallas_call(
        functools.partial(_ffn_kernel, with_attn=with_attn),
        grid=(s // tm,),
        in_specs=specs,
        out_specs=row_spec,
        out_shape=jax.ShapeDtypeStruct((s, D_MODEL), F32),
        scratch_shapes=[
            pltpu.VMEM((2 * FF_CHUNK // LANES, 2 * (FFN_HALO + tm), LANES), F32),
            pltpu.VMEM((2 * FF_CHUNK // LANES, 2 * (FFN_HALO + tm), LANES), F32),
            pltpu.VMEM((D_FF // FF_CHUNK, 2 * FF_CHUNK // LANES, FFN_HALO, LANES), F32),
            pltpu.VMEM((tm, D_MODEL), BF16),
        ],
        compiler_params=pltpu.CompilerParams(
            dimension_semantics=("arbitrary",), vmem_limit_bytes=VMEM_LIMIT),
        name="ffn_attn" if with_attn else "ffn",
    )(*args)


def _qkv_kernel(x_ref, g_ref, w_ref, gain_ref, o_ref, hp_ref, hs_ref, quad_ref):
    j = pl.program_id(1)
    tiles_per_group = 3 * GROUP_WIDTH // QKV_COL_TILE
    tiles_per_kind = GROUP_WIDTH // QKV_COL_TILE

    @pl.when(j == 0)
    def _():
        n_strips = D_MODEL // LANES
        for c in range(ATTN_TILE // NORM_ROW_CHUNK):
            rows = slice(c * NORM_ROW_CHUNK, (c + 1) * NORM_ROW_CHUNK)
            h = _rms(x_ref[rows, :], g_ref[...])
            hp_ref[0, rows, :] = h.astype(BF16)
            for k in range(n_strips):
                hs_ref[k, rows, :] = h[:, k * LANES:(k + 1) * LANES]
        for k in range(n_strips):
            lanes = slice(k * LANES, (k + 1) * LANES)
            quad = quad_ref.at[k % 2]
            for b in range(ATTN_TILE // QUAD):
                for r in range(4):
                    dst = slice(b * QUAD + r * BLOCK, b * QUAD + (r + 1) * BLOCK)
                    v = hs_ref[k, pl.ds(b * QUAD + r, BLOCK, stride=4), :]
                    hp_ref[1, dst, lanes] = v.astype(BF16)
                    quad[dst, :] = v
            for r in range(DILATIONS[2]):
                for b in range(ATTN_TILE // QUAD):
                    src = pl.ds(b * QUAD + (r % 4) * BLOCK + r // 4, BLOCK // 4, stride=4)
                    dst = slice(r * BLOCK + b * (BLOCK // 4), r * BLOCK + (b + 1) * (BLOCK // 4))
                    hp_ref[2, dst, lanes] = quad[src, :].astype(BF16)

    grp = j // tiles_per_group
    kind = (j % tiles_per_group) // tiles_per_kind

    is_normed = kind < 2
    w = w_ref[...].astype(BF16)
    for c in range(ATTN_TILE // QKV_ROW_CHUNK):
        rows = slice(c * QKV_ROW_CHUNK, (c + 1) * QKV_ROW_CHUNK)
        acc = jnp.dot(hp_ref[grp, rows, :], w, preferred_element_type=F32)
        for hh in range(QKV_COL_TILE // HEAD_DIM):
            cols = slice(hh * HEAD_DIM, (hh + 1) * HEAD_DIM)
            a = acc[:, cols]
            inv = lax.rsqrt(jnp.mean(a * a, axis=-1, keepdims=True) + EPS)
            o_ref[rows, cols] = (a * jnp.where(is_normed, inv, 1.0) * gain_ref[:, cols]).astype(BF16)


def _qkv(x, layer, attn_layer, gain, w_qkv, q_norm, k_norm):
    s = x.shape[0]
    width = N_GROUPS * 3 * GROUP_WIDTH
    qn = q_norm[attn_layer].reshape(N_GROUPS, 1, GROUP_WIDTH) * (HEAD_DIM ** -0.5)
    kn = k_norm[attn_layer].reshape(N_GROUPS, 1, GROUP_WIDTH)
    head_gain = jnp.concatenate([qn, kn, jnp.ones_like(kn)], axis=1).reshape(1, width)
    return pl.pallas_call(
        _qkv_kernel,
        grid=(s // ATTN_TILE, width // QKV_COL_TILE),
        in_specs=[
            pl.BlockSpec((ATTN_TILE, D_MODEL), lambda i, j: (i, 0)),
            _layer_spec(layer, (1, D_MODEL)),
            pl.BlockSpec((None, D_MODEL, QKV_COL_TILE), lambda i, j: (attn_layer, 0, j)),
            pl.BlockSpec((1, QKV_COL_TILE), lambda i, j: (0, j)),
        ],
        out_specs=pl.BlockSpec((ATTN_TILE, QKV_COL_TILE), lambda i, j: (i, j)),
        out_shape=jax.ShapeDtypeStruct((s, width), BF16),
        scratch_shapes=[
            pltpu.VMEM((N_GROUPS, ATTN_TILE, D_MODEL), BF16),
            pltpu.VMEM((D_MODEL // LANES, ATTN_TILE, LANES), F32),
            pltpu.VMEM((2, ATTN_TILE, LANES), F32),
        ],
        compiler_params=pltpu.CompilerParams(
            dimension_semantics=("arbitrary", "arbitrary"), vmem_limit_bytes=VMEM_LIMIT),
        name="qkv",
    )(x, gain, w_qkv, head_gain)


def _attn_kernel(slopes_ref, *refs):
    q_refs, kc_refs, kp_refs, vc_refs, vp_refs = (refs[3 * n:3 * n + 3] for n in range(5))
    o_ref, acc_ref, den_ref, max_ref, bias_ref = refs[15:]
    tile = pl.program_id(0)
    head = pl.program_id(1)
    n_blocks = ATTN_TILE // BLOCK

    qi = lax.broadcasted_iota(jnp.int32, (BLOCK, 2 * BLOCK), 0)
    ki = lax.broadcasted_iota(jnp.int32, (BLOCK, 2 * BLOCK), 1)
    delta = qi + BLOCK - ki
    in_window = (delta >= 0) & (delta <= N_BACK)
    has_prev = in_window & ((ki >= BLOCK) | (tile > 0))
    for g, d in enumerate(DILATIONS):
        bias = -slopes_ref[g, head] * (delta * d).astype(F32)
        bias_ref[g, 0] = jnp.where(in_window, bias, 1.0)
        bias_ref[g, 1] = jnp.where(has_prev, bias, 1.0)

    ones = jnp.ones((2 * BLOCK, HEAD_DIM), BF16)

    def store(ref, g, n, val):
        if DILATIONS[g] == 16:
            for b in range(ATTN_TILE // QUAD):
                dst = pl.ds(QUAD * b + BLOCK * (n % 4) + n // 4, BLOCK // 4, stride=4)
                ref[g, dst, :] = val[b * (BLOCK // 4):(b + 1) * (BLOCK // 4), :]
        else:
            ref[g, n * BLOCK:(n + 1) * BLOCK, :] = val

    def scores(g, n):
        d = DILATIONS[g]
        rows = slice(n * BLOCK, (n + 1) * BLOCK)
        if n < d:
            prev_k, prev_v = kp_refs[g][rows, :], vp_refs[g][rows, :]
        else:
            back = slice((n - d) * BLOCK, (n - d + 1) * BLOCK)
            prev_k, prev_v = kc_refs[g][back, :], vc_refs[g][back, :]
        k2 = jnp.concatenate([prev_k, kc_refs[g][rows, :]], axis=0)
        sc = lax.dot_general(q_refs[g][rows, :], k2, (((1,), (1,)), ((), ())), preferred_element_type=F32)
        bias = bias_ref[g, 1 if n < d else 0]
        sc = jnp.where(bias <= 0.0, sc + bias, NEG)
        m = jnp.max(sc, axis=-1, keepdims=True)
        store(max_ref, g, n, jnp.broadcast_to(m, (BLOCK, HEAD_DIM)))
        return jnp.exp(sc - m).astype(BF16), prev_v

    def values(g, n, p, prev_v):
        v2 = jnp.concatenate([prev_v, vc_refs[g][n * BLOCK:(n + 1) * BLOCK, :]], axis=0)
        od = jnp.dot(p, jnp.concatenate([v2, ones], axis=1), preferred_element_type=F32)
        store(acc_ref, g, n, od[:, :HEAD_DIM])
        store(den_ref, g, n, od[:, HEAD_DIM:])

    blocks = [(g, n) for g in range(N_GROUPS) for n in range(n_blocks)]
    items = [blocks[i:i + ATTN_UNROLL] for i in range(0, len(blocks), ATTN_UNROLL)]
    pending = [scores(g, n) for g, n in items[0]]
    for i, item in enumerate(items):
        following = [scores(g, n) for g, n in items[i + 1]] if i + 1 < len(items) else []
        for (g, n), (p, prev_v) in zip(item, pending):
            values(g, n, p, prev_v)
        pending = following

    def merge(c, carry):
        seg = lax.shift_right_logical(c, 2) * QUAD + (c & 3)
        in_position_order = pl.ds(seg, BLOCK, stride=4)
        rows = pl.ds(pl.multiple_of(c * BLOCK, BLOCK), BLOCK)
        pick = lambda ref, g: ref[g, in_position_order if DILATIONS[g] == 1 else rows, :]
        tops = [pick(max_ref, g) for g in range(N_GROUPS)]
        top = functools.reduce(jnp.maximum, tops)
        e = [jnp.exp(t - top) for t in tops]
        num = sum(e[g] * pick(acc_ref, g) for g in range(N_GROUPS))
        den = sum(e[g] * pick(den_ref, g) for g in range(N_GROUPS))
        o_ref[in_position_order, :] = num / den
        return carry

    lax.fori_loop(0, n_blocks, merge, 0)


def _attention(qkv):
    s = qkv.shape[0]
    slopes = jnp.asarray(
        2.0 ** (-ALIBI_MAX * (np.arange(N_GROUPS * HEADS, dtype=np.float32) + 1.0) / (N_GROUPS * HEADS)),
        dtype=F32).reshape(N_GROUPS, HEADS)

    def col(g, kind):
        return lambda t, h: (t, (3 * g + kind) * HEADS + h)

    def prev_col(g, kind):
        ratio = ATTN_TILE // (BLOCK * DILATIONS[g])
        return lambda t, h: (jnp.maximum(t * ratio - 1, 0), (3 * g + kind) * HEADS + h)

    cur = lambda kind: [pl.BlockSpec((ATTN_TILE, HEAD_DIM), col(g, kind)) for g in range(N_GROUPS)]
    prev = lambda kind: [pl.BlockSpec((BLOCK * DILATIONS[g], HEAD_DIM), prev_col(g, kind)) for g in range(N_GROUPS)]
    in_specs = ([pl.BlockSpec(memory_space=pltpu.SMEM)] + cur(0) + cur(1) + prev(1) + cur(2) + prev(2))
    return pl.pallas_call(
        _attn_kernel,
        grid=(s // ATTN_TILE, HEADS),
        in_specs=in_specs,
        out_specs=pl.BlockSpec((ATTN_TILE, HEAD_DIM), lambda t, h: (t, h)),
        out_shape=jax.ShapeDtypeStruct((s, GROUP_WIDTH), F32),
        scratch_shapes=[
            pltpu.VMEM((N_GROUPS, ATTN_TILE, HEAD_DIM), F32),
            pltpu.VMEM((N_GROUPS, ATTN_TILE, HEAD_DIM), F32),
            pltpu.VMEM((N_GROUPS, ATTN_TILE, HEAD_DIM), F32),
            pltpu.VMEM((N_GROUPS, 2, BLOCK, 2 * BLOCK), F32),
        ],
        compiler_params=pltpu.CompilerParams(
            dimension_semantics=("arbitrary", "arbitrary"), vmem_limit_bytes=VMEM_LIMIT),
        name="dilated_attention",
    )(slopes, *([qkv] * 15))


def kernel(x, norm_mix, norm_ffn, cm_w_in, cm_b_in, cm_dw, cm_dw_b, cm_ln_g, cm_ln_b, cm_w_out, cm_b_out,
           at_w_qkv, at_q_norm, at_k_norm, at_w_out, ff_w_up, ff_dw, ff_dw_b, ff_w_down):
    batch, seq, _ = x.shape
    assert seq % ATTN_TILE == 0 and x.shape[-1] == D_MODEL
    vec = lambda p: p.reshape(p.shape[0], 1, -1)
    bf16 = lambda p: p.astype(BF16)
    norm_mix, norm_ffn = vec(norm_mix), vec(norm_ffn)
    conformer_params = (bf16(cm_w_in), vec(cm_b_in), cm_dw, vec(cm_dw_b), vec(cm_ln_g), vec(cm_ln_b),
                        bf16(cm_w_out), vec(cm_b_out))
    ffn_params = (bf16(ff_w_up), ff_dw, vec(ff_dw_b), bf16(ff_w_down))
    w_attn_out = bf16(at_w_out)
    outs = []
    for b in range(batch):
        h = x.reshape(seq, D_MODEL) if batch == 1 else x[b]
        h = _conformer(h, 0, 0, norm_mix, *conformer_params)
        h = _ffn(h, 0, norm_ffn, *ffn_params)
        qkv = _qkv(h, 1, 0, norm_mix, at_w_qkv, at_q_norm, at_k_norm)
        attn = _attention(qkv)
        h = _ffn(h, 1, norm_ffn, *ffn_params, attn=attn, attn_layer=0, w_attn=w_attn_out)
        outs.append(h)
    return outs[0].reshape(1, seq, D_MODEL) if batch == 1 else jnp.stack(outs, axis=0)
```

```python
import functools

import numpy as np
import jax
import jax.numpy as jnp
from jax import lax
from jax.experimental import pallas as pl
from jax.experimental.pallas import tpu as pltpu

F32 = jnp.float32
BF16 = jnp.bfloat16

D_MODEL = 1024
HEAD_DIM = 128
HEADS = 8
GROUP_WIDTH = HEADS * HEAD_DIM
DILATIONS = (1, 4, 16)
N_GROUPS = len(DILATIONS)
N_BACK = 128
BLOCK = 128
ATTN_TILE = BLOCK * DILATIONS[-1]
QUAD = 4 * BLOCK
ALIBI_MAX = 8.0
CONV_KERNEL = 31
CONV_HALO = 32
FFN_KERNEL = 3
FFN_HALO = 8
D_FF = 2816
FF_CHUNK = 256
EPS = 1e-6
NEG = -1e30
LANES = 128
SUBLANES = 8
VMEM_LIMIT = 60 * 1024 * 1024

CONF_ROW_TILE = 1024
CONF_SUB = 256
CONV_ROWS = 64
FFN_ROW_TILE = 1024
QKV_COL_TILE = 1024
QKV_ROW_CHUNK = 256
NORM_ROW_CHUNK = 256
ATTN_UNROLL = 4


def _rms(x, gain):
    return x * lax.rsqrt(jnp.mean(x * x, axis=-1, keepdims=True) + EPS) * gain


def _spaced(start, size):
    return pl.ds(2 * start, size, stride=2)


def _layer_spec(layer, shape):
    index = (layer,) + (0,) * len(shape)
    return pl.BlockSpec((None, *shape), lambda *_: index, pipeline_mode=pl.Buffered(1))


def _conformer_kernel(x_ref, gmix_ref, win_ref, bin_ref, dw_ref, dwb_ref, lng_ref, lnb_ref, wout_ref, bout_ref,
                      o_ref, buf_ref, y_ref):
    tm = x_ref.shape[0]
    n_sub = tm // CONF_SUB

    @pl.when(pl.program_id(0) == 0)
    def _():
        for c in range(D_MODEL // LANES):
            buf_ref[c, _spaced(0, CONV_HALO), :] = jnp.zeros((CONV_HALO, LANES), F32)

    def glu(s):
        first = pl.multiple_of(s * CONF_SUB, CONF_SUB)
        h = _rms(x_ref[pl.ds(first, CONF_SUB), :], gmix_ref[...]).astype(BF16)
        u = jnp.dot(h, win_ref[...], preferred_element_type=F32) + bin_ref[...]
        g = u[:, :D_MODEL] * jax.nn.sigmoid(u[:, D_MODEL:])
        for c in range(D_MODEL // LANES):
            buf_ref[c, _spaced(CONV_HALO + first, CONF_SUB), :] = g[:, c * LANES:(c + 1) * LANES]

    def conv(s):
        reach = SUBLANES * ((CONV_KERNEL - 1) // SUBLANES)
        for c in range(D_MODEL // LANES):
            lanes = slice(c * LANES, (c + 1) * LANES)
            for q in range(CONF_SUB // CONV_ROWS):
                first = pl.multiple_of(s * CONF_SUB + q * CONV_ROWS, CONV_ROWS)
                y = dwb_ref[:, lanes]
                for j in range(SUBLANES):
                    window = buf_ref[c, _spaced(CONV_HALO + first - j - reach, CONV_ROWS + reach), :]
                    for m in range(j, CONV_KERNEL, SUBLANES):
                        w = dw_ref[CONV_KERNEL - 1 - m:CONV_KERNEL - m, lanes]
                        y = y + window[reach - (m - j):reach - (m - j) + CONV_ROWS, :] * w
                y_ref[pl.ds(first, CONV_ROWS), lanes] = y

    def project(s):
        rows = pl.ds(pl.multiple_of(s * CONF_SUB, CONF_SUB), CONF_SUB)
        y = y_ref[rows, :]
        mu = jnp.mean(y, axis=-1, keepdims=True)
        yc = y - mu
        var = jnp.mean(yc * yc, axis=-1, keepdims=True)
        z = yc * lax.rsqrt(var + EPS) * lng_ref[...] + lnb_ref[...]
        z = (z * jax.nn.sigmoid(z)).astype(BF16)
        o_ref[rows, :] = x_ref[rows, :] + jnp.dot(z, wout_ref[...], preferred_element_type=F32) + bout_ref[...]

    glu(0)

    def body(s, carry):
        conv(s)
        glu(s + 1)
        project(s)
        return carry

    lax.fori_loop(0, n_sub - 1, body, 0)
    conv(n_sub - 1)
    project(n_sub - 1)
    for c in range(D_MODEL // LANES):
        buf_ref[c, _spaced(0, CONV_HALO), :] = buf_ref[c, _spaced(tm, CONV_HALO), :]


def _conformer(x, layer, conv_layer, gmix, w_in, b_in, dw, dw_b, ln_g, ln_b, w_out, b_out):
    s = x.shape[0]
    tm = CONF_ROW_TILE
    return pl.pallas_call(
        _conformer_kernel,
        grid=(s // tm,),
        in_specs=[
            pl.BlockSpec((tm, D_MODEL), lambda i: (i, 0)),
            _layer_spec(layer, (1, D_MODEL)),
            _layer_spec(conv_layer, (D_MODEL, 2 * D_MODEL)),
            _layer_spec(conv_layer, (1, 2 * D_MODEL)),
            _layer_spec(conv_layer, (CONV_KERNEL, D_MODEL)),
            _layer_spec(conv_layer, (1, D_MODEL)),
            _layer_spec(conv_layer, (1, D_MODEL)),
            _layer_spec(conv_layer, (1, D_MODEL)),
            _layer_spec(conv_layer, (D_MODEL, D_MODEL)),
            _layer_spec(conv_layer, (1, D_MODEL)),
        ],
        out_specs=pl.BlockSpec((tm, D_MODEL), lambda i: (i, 0)),
        out_shape=jax.ShapeDtypeStruct((s, D_MODEL), F32),
        scratch_shapes=[
            pltpu.VMEM((D_MODEL // LANES, 2 * (CONV_HALO + tm), LANES), F32),
            pltpu.VMEM((tm, D_MODEL), F32),
        ],
        compiler_params=pltpu.CompilerParams(
            dimension_semantics=("arbitrary",), vmem_limit_bytes=VMEM_LIMIT),
        name="conformer",
    )(x, gmix, w_in, b_in, dw, dw_b, ln_g, ln_b, w_out, b_out)


def _ffn_kernel(*refs, with_attn):
    if with_attn:
        x_ref, a_ref, wattn_ref, *refs = refs
    else:
        x_ref, *refs = refs
    g_ref, wup_ref, dw_ref, dwb_ref, wdown_ref, o_ref, ubuf_a, ubuf_b, carry_ref, h_ref = refs
    tm = x_ref.shape[0]
    n_chunks = D_FF // FF_CHUNK

    @pl.when(pl.program_id(0) == 0)
    def _():
        carry_ref[...] = jnp.zeros(carry_ref.shape, F32)

    x = x_ref[...]
    if with_attn:
        x = x + jnp.dot(a_ref[...].astype(BF16), wattn_ref[...], preferred_element_type=F32)
    o_ref[...] = x
    h_ref[...] = _rms(x, g_ref[...]).astype(BF16)

    def gate_value(ref, rows, c):
        gate = pl.ds(pl.multiple_of(c * FF_CHUNK, FF_CHUNK), FF_CHUNK)
        value = pl.ds(pl.multiple_of(D_FF + c * FF_CHUNK, FF_CHUNK), FF_CHUNK)
        return jnp.concatenate([ref[rows, gate], ref[rows, value]], axis=1)

    n_strips = 2 * FF_CHUNK // LANES

    def up(c, ubuf_ref):
        u = jnp.dot(h_ref[...], gate_value(wup_ref, slice(None), c), preferred_element_type=F32)
        for k in range(n_strips):
            strip = u[:, k * LANES:(k + 1) * LANES]
            ubuf_ref[k, _spaced(0, FFN_HALO), :] = carry_ref[c, k]
            ubuf_ref[k, _spaced(FFN_HALO, tm), :] = strip
            carry_ref[c, k] = strip[tm - FFN_HALO:tm, :]

    def down(c, ubuf_ref):
        bias = gate_value(dwb_ref, slice(None), c)
        taps = [gate_value(dw_ref, slice(j, j + 1), c) for j in range(FFN_KERNEL)]
        strips = []
        for k in range(n_strips):
            lanes = slice(k * LANES, (k + 1) * LANES)
            y = bias[:, lanes]
            for j in range(FFN_KERNEL):
                y = y + ubuf_ref[k, _spaced(FFN_HALO - (FFN_KERNEL - 1 - j), tm), :] * taps[j][:, lanes]
            strips.append(y)
        y = jnp.concatenate(strips, axis=1)
        gate = y[:, :FF_CHUNK]
        act = (gate * jax.nn.sigmoid(gate) * y[:, FF_CHUNK:]).astype(BF16)
        rows = pl.ds(pl.multiple_of(c * FF_CHUNK, FF_CHUNK), FF_CHUNK)
        o_ref[...] += jnp.dot(act, wdown_ref[rows, :], preferred_element_type=F32)

    up(0, ubuf_a)

    def pair(it, carry):
        c = 2 * it
        up(c + 1, ubuf_b)
        down(c, ubuf_a)
        up(c + 2, ubuf_a)
        down(c + 1, ubuf_b)
        return carry

    assert n_chunks % 2 == 1
    lax.fori_loop(0, n_chunks // 2, pair, 0)
    down(n_chunks - 1, ubuf_a)


def _ffn(x, layer, gain, w_up, dw, dw_b, w_down, attn=None, attn_layer=None, w_attn=None):
    s = x.shape[0]
    tm = FFN_ROW_TILE
    with_attn = attn is not None
    row_spec = pl.BlockSpec((tm, D_MODEL), lambda i: (i, 0))
    args, specs = [x], [row_spec]
    if with_attn:
        args += [attn, w_attn]
        specs += [pl.BlockSpec((tm, GROUP_WIDTH), lambda i: (i, 0)),
                  _layer_spec(attn_layer, (GROUP_WIDTH, D_MODEL))]
    args += [gain, w_up, dw, dw_b, w_down]
    specs += [_layer_spec(layer, (1, D_MODEL)), _layer_spec(layer, (D_MODEL, 2 * D_FF)),
              _layer_spec(layer, (FFN_KERNEL, 2 * D_FF)), _layer_spec(layer, (1, 2 * D_FF)),
              _layer_spec(layer, (D_FF, D_MODEL))]
    return pl.pallas_call(
        functools.partial(_ffn_kernel, with_attn=with_attn),
        grid=(s // tm,),
        in_specs=specs,
        out_specs=row_spec,
        out_shape=jax.ShapeDtypeStruct((s, D_MODEL), F32),
        scratch_shapes=[
            pltpu.VMEM((2 * FF_CHUNK // LANES, 2 * (FFN_HALO + tm), LANES), F32),
            pltpu.VMEM((2 * FF_CHUNK // LANES, 2 * (FFN_HALO + tm), LANES), F32),
            pltpu.VMEM((D_FF // FF_CHUNK, 2 * FF_CHUNK // LANES, FFN_HALO, LANES), F32),
            pltpu.VMEM((tm, D_MODEL), BF16),
        ],
        compiler_params=pltpu.CompilerParams(
            dimension_semantics=("arbitrary",), vmem_limit_bytes=VMEM_LIMIT),
        name="ffn_attn" if with_attn else "ffn",
    )(*args)


def _qkv_kernel(x_ref, g_ref, w_ref, gain_ref, o_ref, hp_ref, hs_ref, quad_ref):
    j = pl.program_id(1)
    tiles_per_group = 3 * GROUP_WIDTH // QKV_COL_TILE
    tiles_per_kind = GROUP_WIDTH // QKV_COL_TILE

    @pl.when(j == 0)
    def _():
        n_strips = D_MODEL // LANES
        for c in range(ATTN_TILE // NORM_ROW_CHUNK):
            rows = slice(c * NORM_ROW_CHUNK, (c + 1) * NORM_ROW_CHUNK)
            h = _rms(x_ref[rows, :], g_ref[...])
            hp_ref[0, rows, :] = h.astype(BF16)
            for k in range(n_strips):
                hs_ref[k, rows, :] = h[:, k * LANES:(k + 1) * LANES]
        for k in range(n_strips):
            lanes = slice(k * LANES, (k + 1) * LANES)
            quad = quad_ref.at[k % 2]
            for b in range(ATTN_TILE // QUAD):
                for r in range(4):
                    dst = slice(b * QUAD + r * BLOCK, b * QUAD + (r + 1) * BLOCK)
                    v = hs_ref[k, pl.ds(b * QUAD + r, BLOCK, stride=4), :]
                    hp_ref[1, dst, lanes] = v.astype(BF16)
                    quad[dst, :] = v
            for r in range(DILATIONS[2]):
                for b in range(ATTN_TILE // QUAD):
                    src = pl.ds(b * QUAD + (r % 4) * BLOCK + r // 4, BLOCK // 4, stride=4)
                    dst = slice(r * BLOCK + b * (BLOCK // 4), r * BLOCK + (b + 1) * (BLOCK // 4))
                    hp_ref[2, dst, lanes] = quad[src, :].astype(BF16)

    grp = j // tiles_per_group
    kind = (j % tiles_per_group) // tiles_per_kind

    is_normed = kind < 2
    w = w_ref[...].astype(BF16)
    for c in range(ATTN_TILE // QKV_ROW_CHUNK):
        rows = slice(c * QKV_ROW_CHUNK, (c + 1) * QKV_ROW_CHUNK)
        acc = jnp.dot(hp_ref[grp, rows, :], w, preferred_element_type=F32)
        for hh in range(QKV_COL_TILE // HEAD_DIM):
            cols = slice(hh * HEAD_DIM, (hh + 1) * HEAD_DIM)
            a = acc[:, cols]
            inv = lax.rsqrt(jnp.mean(a * a, axis=-1, keepdims=True) + EPS)
            o_ref[rows, cols] = (a * jnp.where(is_normed, inv, 1.0) * gain_ref[:, cols]).astype(BF16)


def _qkv(x, layer, attn_layer, gain, w_qkv, q_norm, k_norm):
    s = x.shape[0]
    width = N_GROUPS * 3 * GROUP_WIDTH
    qn = q_norm[attn_layer].reshape(N_GROUPS, 1, GROUP_WIDTH) * (HEAD_DIM ** -0.5)
    kn = k_norm[attn_layer].reshape(N_GROUPS, 1, GROUP_WIDTH)
    head_gain = jnp.concatenate([qn, kn, jnp.ones_like(kn)], axis=1).reshape(1, width)
    return pl.pallas_call(
        _qkv_kernel,
        grid=(s // ATTN_TILE, width // QKV_COL_TILE),
        in_specs=[
            pl.BlockSpec((ATTN_TILE, D_MODEL), lambda i, j: (i, 0)),
            _layer_spec(layer, (1, D_MODEL)),
            pl.BlockSpec((None, D_MODEL, QKV_COL_TILE), lambda i, j: (attn_layer, 0, j)),
            pl.BlockSpec((1, QKV_COL_TILE), lambda i, j: (0, j)),
        ],
        out_specs=pl.BlockSpec((ATTN_TILE, QKV_COL_TILE), lambda i, j: (i, j)),
        out_shape=jax.ShapeDtypeStruct((s, width), BF16),
        scratch_shapes=[
            pltpu.VMEM((N_GROUPS, ATTN_TILE, D_MODEL), BF16),
            pltpu.VMEM((D_MODEL // LANES, ATTN_TILE, LANES), F32),
            pltpu.VMEM((2, ATTN_TILE, LANES), F32),
        ],
        compiler_params=pltpu.CompilerParams(
            dimension_semantics=("arbitrary", "arbitrary"), vmem_limit_bytes=VMEM_LIMIT),
        name="qkv",
    )(x, gain, w_qkv, head_gain)


def _attn_kernel(slopes_ref, *refs):
    q_refs, kc_refs, kp_refs, vc_refs, vp_refs = (refs[3 * n:3 * n + 3] for n in range(5))
    o_ref, acc_ref, den_ref, max_ref, bias_ref = refs[15:]
    tile = pl.program_id(0)
    head = pl.program_id(1)
    n_blocks = ATTN_TILE // BLOCK

    qi = lax.broadcasted_iota(jnp.int32, (BLOCK, 2 * BLOCK), 0)
    ki = lax.broadcasted_iota(jnp.int32, (BLOCK, 2 * BLOCK), 1)
    delta = qi + BLOCK - ki
    in_window = (delta >= 0) & (delta <= N_BACK)
    has_prev = in_window & ((ki >= BLOCK) | (tile > 0))
    for g, d in enumerate(DILATIONS):
        bias = -slopes_ref[g, head] * (delta * d).astype(F32)
        bias_ref[g, 0] = jnp.where(in_window, bias, 1.0)
        bias_ref[g, 1] = jnp.where(has_prev, bias, 1.0)

    ones = jnp.ones((2 * BLOCK, HEAD_DIM), BF16)

    def store(ref, g, n, val):
        if DILATIONS[g] == 16:
            for b in range(ATTN_TILE // QUAD):
                dst = pl.ds(QUAD * b + BLOCK * (n % 4) + n // 4, BLOCK // 4, stride=4)
                ref[g, dst, :] = val[b * (BLOCK // 4):(b + 1) * (BLOCK // 4), :]
        else:
            ref[g, n * BLOCK:(n + 1) * BLOCK, :] = val

    def scores(g, n):
        d = DILATIONS[g]
        rows = slice(n * BLOCK, (n + 1) * BLOCK)
        if n < d:
            prev_k, prev_v = kp_refs[g][rows, :], vp_refs[g][rows, :]
        else:
            back = slice((n - d) * BLOCK, (n - d + 1) * BLOCK)
            prev_k, prev_v = kc_refs[g][back, :], vc_refs[g][back, :]
        k2 = jnp.concatenate([prev_k, kc_refs[g][rows, :]], axis=0)
        sc = lax.dot_general(q_refs[g][rows, :], k2, (((1,), (1,)), ((), ())), preferred_element_type=F32)
        bias = bias_ref[g, 1 if n < d else 0]
        sc = jnp.where(bias <= 0.0, sc + bias, NEG)
        m = jnp.max(sc, axis=-1, keepdims=True)
        store(max_ref, g, n, jnp.broadcast_to(m, (BLOCK, HEAD_DIM)))
        return jnp.exp(sc - m).astype(BF16), prev_v

    def values(g, n, p, prev_v):
        v2 = jnp.concatenate([prev_v, vc_refs[g][n * BLOCK:(n + 1) * BLOCK, :]], axis=0)
        od = jnp.dot(p, jnp.concatenate([v2, ones], axis=1), preferred_element_type=F32)
        store(acc_ref, g, n, od[:, :HEAD_DIM])
        store(den_ref, g, n, od[:, HEAD_DIM:])

    blocks = [(g, n) for g in range(N_GROUPS) for n in range(n_blocks)]
    items = [blocks[i:i + ATTN_UNROLL] for i in range(0, len(blocks), ATTN_UNROLL)]
    pending = [scores(g, n) for g, n in items[0]]
    for i, item in enumerate(items):
        following = [scores(g, n) for g, n in items[i + 1]] if i + 1 < len(items) else []
        for (g, n), (p, prev_v) in zip(item, pending):
            values(g, n, p, prev_v)
        pending = following

    def merge(c, carry):
        seg = lax.shift_right_logical(c, 2) * QUAD + (c & 3)
        in_position_order = pl.ds(seg, BLOCK, stride=4)
        rows = pl.ds(pl.multiple_of(c * BLOCK, BLOCK), BLOCK)
        pick = lambda ref, g: ref[g, in_position_order if DILATIONS[g] == 1 else rows, :]
        tops = [pick(max_ref, g) for g in range(N_GROUPS)]
        top = functools.reduce(jnp.maximum, tops)
        e = [jnp.exp(t - top) for t in tops]
        num = sum(e[g] * pick(acc_ref, g) for g in range(N_GROUPS))
        den = sum(e[g] * pick(den_ref, g) for g in range(N_GROUPS))
        o_ref[in_position_order, :] = num / den
        return carry

    lax.fori_loop(0, n_blocks, merge, 0)


def _attention(qkv):
    s = qkv.shape[0]
    slopes = jnp.asarray(
        2.0 ** (-ALIBI_MAX * (np.arange(N_GROUPS * HEADS, dtype=np.float32) + 1.0) / (N_GROUPS * HEADS)),
        dtype=F32).reshape(N_GROUPS, HEADS)

    def col(g, kind):
        return lambda t, h: (t, (3 * g + kind) * HEADS + h)

    def prev_col(g, kind):
        ratio = ATTN_TILE // (BLOCK * DILATIONS[g])
        return lambda t, h: (jnp.maximum(t * ratio - 1, 0), (3 * g + kind) * HEADS + h)

    cur = lambda kind: [pl.BlockSpec((ATTN_TILE, HEAD_DIM), col(g, kind)) for g in range(N_GROUPS)]
    prev = lambda kind: [pl.BlockSpec((BLOCK * DILATIONS[g], HEAD_DIM), prev_col(g, kind)) for g in range(N_GROUPS)]
    in_specs = ([pl.BlockSpec(memory_space=pltpu.SMEM)] + cur(0) + cur(1) + prev(1) + cur(2) + prev(2))
    return pl.pallas_call(
        _attn_kernel,
        grid=(s // ATTN_TILE, HEADS),
        in_specs=in_specs,
        out_specs=pl.BlockSpec((ATTN_TILE, HEAD_DIM), lambda t, h: (t, h)),
        out_shape=jax.ShapeDtypeStruct((s, GROUP_WIDTH), F32),
        scratch_shapes=[
            pltpu.VMEM((N_GROUPS, ATTN_TILE, HEAD_DIM), F32),
            pltpu.VMEM((N_GROUPS, ATTN_TILE, HEAD_DIM), F32),
            pltpu.VMEM((N_GROUPS, ATTN_TILE, HEAD_DIM), F32),
            pltpu.VMEM((N_GROUPS, 2, BLOCK, 2 * BLOCK), F32),
        ],
        compiler_params=pltpu.CompilerParams(
            dimension_semantics=("arbitrary", "arbitrary"), vmem_limit_bytes=VMEM_LIMIT),
        name="dilated_attention",
    )(slopes, *([qkv] * 15))


def kernel(x, norm_mix, norm_ffn, cm_w_in, cm_b_in, cm_dw, cm_dw_b, cm_ln_g, cm_ln_b, cm_w_out, cm_b_out,
           at_w_qkv, at_q_norm, at_k_norm, at_w_out, ff_w_up, ff_dw, ff_dw_b, ff_w_down):
    batch, seq, _ = x.shape
    assert seq % ATTN_TILE == 0 and x.shape[-1] == D_MODEL
    vec = lambda p: p.reshape(p.shape[0], 1, -1)
    bf16 = lambda p: p.astype(BF16)
    norm_mix, norm_ffn = vec(norm_mix), vec(norm_ffn)
    conformer_params = (bf16(cm_w_in), vec(cm_b_in), cm_dw, vec(cm_dw_b), vec(cm_ln_g), vec(cm_ln_b),
                        bf16(cm_w_out), vec(cm_b_out))
    ffn_params = (bf16(ff_w_up), ff_dw, vec(ff_dw_b), bf16(ff_w_down))
    w_attn_out = bf16(at_w_out)
    outs = []
    for b in range(batch):
        h = x.reshape(seq, D_MODEL) if batch == 1 else x[b]
        h = _conformer(h, 0, 0, norm_mix, *conformer_params)
        h = _ffn(h, 0, norm_ffn, *ffn_params)
        qkv = _qkv(h, 1, 0, norm_mix, at_w_qkv, at_q_norm, at_k_norm)
        attn = _attention(qkv)
        h = _ffn(h, 1, norm_ffn, *ffn_params, attn=attn, attn_layer=0, w_attn=w_attn_out)
        outs.append(h)
    return outs[0].reshape(1, seq, D_MODEL) if batch == 1 else jnp.stack(outs, axis=0)
```
